```python
import math
import jax, jax.numpy as jnp
from jax import lax
import numpy as np

D_MODEL = 1024
BATCH = 8
SEQ = 4096
DEPTH = 2

D_ATTN = D_MODEL // 2
HEAD_DIM = 64
N_HEADS_A = D_ATTN // HEAD_DIM
ROT_DIM = HEAD_DIM // 4
ROPE_THETA = 500000.0
DILATED_PATTERNS = ((128, 1), (512, 4), (2048, 16))
D_POOL = D_MODEL - D_ATTN
POOL_WINDOWS = (2, 4, 8, 16)
N_POOL_GROUPS = len(POOL_WINDOWS)
POOL_C = D_POOL // N_POOL_GROUPS
D_IN_EVEN = 3 * D_ATTN + D_POOL
S5_GROUP = 16
S5_GROUPS = D_MODEL // S5_GROUP
S5_STATE = 64
D_FF = ((8 * D_MODEL // 3 + 255) // 256) * 256
N_EVEN = (DEPTH + 1) // 2
N_ODD = DEPTH // 2
EPS = 1e-6

kernel_name = 'hybrid_dilated_pool_s5_macaron'


def _rmsnorm(x, g):
    xf = x.astype(jnp.float32)
    y = xf * lax.rsqrt(jnp.mean(xf * xf, axis=-1, keepdims=True) + EPS)
    return (y * g.astype(jnp.float32)).astype(x.dtype)


def _swiglu(h, w_gate, w_up, w_down):
    return (jax.nn.silu(h @ w_gate) * (h @ w_up)) @ w_down


def _rotary_tables(positions):
    inv_freq = ROPE_THETA ** (-jnp.arange(0, ROT_DIM, 2, dtype=jnp.float32) / ROT_DIM)
    ang = positions.astype(jnp.float32)[..., None] * inv_freq
    return jnp.cos(ang)[:, :, None, :], jnp.sin(ang)[:, :, None, :]


def _partial_rotary(t, cos, sin):
    half = ROT_DIM // 2
    tf = t[..., :ROT_DIM].astype(jnp.float32)
    t1, t2 = tf[..., :half], tf[..., half:]
    rot = jnp.concatenate([t1 * cos - t2 * sin, t2 * cos + t1 * sin], axis=-1).astype(t.dtype)
    return jnp.concatenate([rot, t[..., ROT_DIM:]], axis=-1)


def _dilated_window_attention(q, k, v, window, dilation):
    Bsz, S, H, Dh = q.shape
    n_back = window // dilation
    blk = n_back
    L = S // dilation
    nb = -(-L // blk)
    Lp = nb * blk

    def to_sub(t):
        return t.reshape(Bsz, L, dilation, H, Dh).transpose(0, 2, 3, 1, 4)

    qs = jnp.pad(to_sub(q), ((0, 0), (0, 0), (0, 0), (0, Lp - L), (0, 0)))
    qs = qs.reshape(Bsz, dilation, H, nb, blk, Dh)

    def windows(t):
        t = jnp.pad(to_sub(t), ((0, 0), (0, 0), (0, 0), (blk, Lp - L), (0, 0)))
        t = t.reshape(Bsz, dilation, H, nb + 1, blk, Dh)
        return jnp.concatenate([t[:, :, :, :-1], t[:, :, :, 1:]], axis=-2)

    kw, vw = windows(k), windows(v)
    scores = jnp.einsum('bdhnqe,bdhnke->bdhnqk', qs, kw,
                        preferred_element_type=jnp.float32) * (Dh ** -0.5)
    qi = jnp.arange(blk)[:, None]
    kj = jnp.arange(2 * blk)[None, :]
    delta = qi - kj + blk
    blk_idx = jnp.arange(nb)[:, None, None]
    valid = (delta >= 0) & (delta <= n_back) & (blk_idx * blk - blk + kj >= 0)
    scores = jnp.where(valid, scores, -jnp.inf)
    lse = jax.nn.logsumexp(scores, axis=-1)
    probs = jnp.exp(scores - lse[..., None])
    out = jnp.einsum('bdhnqk,bdhnke->bdhnqe', probs.astype(vw.dtype), vw)
    out = out.reshape(Bsz, dilation, H, Lp, Dh)[:, :, :, :L]
    out = out.transpose(0, 3, 1, 2, 4).reshape(Bsz, S, H, Dh)
    lse = lse.reshape(Bsz, dilation, H, Lp)[..., :L].transpose(0, 3, 1, 2).reshape(Bsz, S, H)
    return out, lse


def _multiscale_pool(p, pool_w, pool_scale):
    Bsz, S, _ = p.shape
    pf = p.astype(jnp.float32).reshape(Bsz, S, N_POOL_GROUPS, POOL_C)
    cs0 = jnp.pad(jnp.cumsum(pf, axis=1), ((0, 0), (1, 0), (0, 0), (0, 0)))
    t1 = jnp.arange(1, S + 1)
    pooled = []
    for g, w in enumerate(POOL_WINDOWS):
        upper = cs0[:, 1:, g]
        lower = jnp.pad(cs0[:, :S + 1 - w, g], ((0, 0), (w - 1, 0), (0, 0)))
        count = jnp.minimum(t1, w).astype(jnp.float32)[None, :, None]
        pooled.append((upper - lower) / count - pf[:, :, g])
    pooled = jnp.stack(pooled, axis=2)
    y = jnp.einsum('bsgc,gce->bsge', pooled, pool_w.astype(jnp.float32)).reshape(Bsz, S, D_POOL)
    return (y * pool_scale.astype(jnp.float32)).astype(p.dtype)


def _even_mixer(h, cos, sin, w_in, q_norm, k_norm, pool_w, pool_scale, w_out):
    Bsz, S, _ = h.shape
    proj = h @ w_in
    q, k, v, p = jnp.split(proj, [D_ATTN, 2 * D_ATTN, 3 * D_ATTN], axis=-1)
    shp = (Bsz, S, N_HEADS_A, HEAD_DIM)
    q = _partial_rotary(_rmsnorm(q.reshape(shp), q_norm), cos, sin)
    k = _partial_rotary(_rmsnorm(k.reshape(shp), k_norm), cos, sin)
    v = v.reshape(shp)
    outs, lses = [], []
    for window, dilation in DILATED_PATTERNS:
        o, l = _dilated_window_attention(q, k, v, window, dilation)
        outs.append(o)
        lses.append(l)
    wts = jax.nn.softmax(jnp.stack(lses, axis=0), axis=0)
    attn = jnp.einsum('pbsh,pbshe->bshe', wts, jnp.stack(outs, axis=0).astype(jnp.float32))
    attn = attn.reshape(Bsz, S, D_ATTN).astype(h.dtype)
    pool = _multiscale_pool(p, pool_w, pool_scale)
    return jnp.concatenate([attn, pool], axis=-1) @ w_out


def _s5_ssm(u, a_re, a_im, log_dt, b_re, b_im, c_re, c_im, d_skip):
    Bsz, S, _ = u.shape
    f32 = jnp.float32
    uf = u.astype(f32).reshape(Bsz, S, S5_GROUPS, S5_GROUP)
    lam_re = jnp.minimum(a_re.astype(f32), -1e-4)
    lam_im = a_im.astype(f32)
    dt = jnp.exp(log_dt.astype(f32))[:, None]
    mag = jnp.exp(lam_re * dt)
    lbar_re = mag * jnp.cos(lam_im * dt)
    lbar_im = mag * jnp.sin(lam_im * dt)
    den = lam_re * lam_re + lam_im * lam_im
    nre = lbar_re - 1.0
    coef_re = (nre * lam_re + lbar_im * lam_im) / den
    coef_im = (lbar_im * lam_re - nre * lam_im) / den
    br, bi = b_re.astype(f32), b_im.astype(f32)
    bb_re = coef_re[..., None] * br - coef_im[..., None] * bi
    bb_im = coef_re[..., None] * bi + coef_im[..., None] * br
    bu_re = jnp.einsum('bsgc,gpc->bsgp', uf, bb_re)
    bu_im = jnp.einsum('bsgc,gpc->bsgp', uf, bb_im)
    a_seq_re = jnp.broadcast_to(lbar_re, (1, S, S5_GROUPS, S5_STATE))
    a_seq_im = jnp.broadcast_to(lbar_im, (1, S, S5_GROUPS, S5_STATE))

    def combine(e1, e2):
        ar1, ai1, br1, bi1 = e1
        ar2, ai2, br2, bi2 = e2
        return (ar2 * ar1 - ai2 * ai1,
                ar2 * ai1 + ai2 * ar1,
                ar2 * br1 - ai2 * bi1 + br2,
                ar2 * bi1 + ai2 * br1 + bi2)

    _, _, h_re, h_im = lax.associative_scan(combine, (a_seq_re, a_seq_im, bu_re, bu_im), axis=1)
    y = (jnp.einsum('bsgp,gcp->bsgc', h_re, c_re.astype(f32))
         - jnp.einsum('bsgp,gcp->bsgc', h_im, c_im.astype(f32)))
    return y.reshape(Bsz, S, D_MODEL) + d_skip.astype(f32) * uf.reshape(Bsz, S, D_MODEL)


def _odd_mixer(h, a_re, a_im, log_dt, b_re, b_im, c_re, c_im, d_skip, w_glu):
    y = _s5_ssm(h, a_re, a_im, log_dt, b_re, b_im, c_re, c_im, d_skip)
    z = jax.nn.gelu(y).astype(h.dtype)
    val, gate = jnp.split(z @ w_glu, 2, axis=-1)
    return val * jax.nn.sigmoid(gate)


def setup_inputs(seed: int = 0) -> dict:
    key = jax.random.key(seed)
    ks = jax.random.split(key, 24)
    f32 = jnp.float32

    def nrm(k, shape, scale):
        return scale * jax.random.normal(k, shape, f32)

    x = nrm(ks[0], (BATCH, SEQ, D_MODEL), 1.0)
    positions = (jax.random.randint(ks[1], (BATCH, 1), 0, 1024, jnp.int32)
                 + jnp.arange(SEQ, dtype=jnp.int32)[None, :])
    ffn_norm = 1.0 + nrm(ks[2], (DEPTH, 2, D_MODEL), 0.02)
    ffn_w_gate = nrm(ks[3], (DEPTH, 2, D_MODEL, D_FF), D_MODEL ** -0.5)
    ffn_w_up = nrm(ks[4], (DEPTH, 2, D_MODEL, D_FF), D_MODEL ** -0.5)
    ffn_w_down = nrm(ks[5], (DEPTH, 2, D_FF, D_MODEL), D_FF ** -0.5)
    mix_norm = 1.0 + nrm(ks[6], (DEPTH, D_MODEL), 0.02)
    ev_w_in = nrm(ks[7], (N_EVEN, D_MODEL, D_IN_EVEN), D_MODEL ** -0.5)
    ev_q_norm = 1.0 + nrm(ks[8], (N_EVEN, HEAD_DIM), 0.02)
    ev_k_norm = 1.0 + nrm(ks[9], (N_EVEN, HEAD_DIM), 0.02)
    ev_pool_w = nrm(ks[10], (N_EVEN, N_POOL_GROUPS, POOL_C, POOL_C), POOL_C ** -0.5)
    ev_pool_scale = 1.0 + nrm(ks[11], (N_EVEN, D_POOL), 0.02)
    ev_w_out = nrm(ks[12], (N_EVEN, D_MODEL, D_MODEL), D_MODEL ** -0.5)
    s5_a_re = -0.5 + nrm(ks[13], (N_ODD, S5_GROUPS, S5_STATE), 0.01)
    s5_a_im = (math.pi * jnp.arange(S5_STATE, dtype=f32))[None, None, :] + nrm(ks[14], (N_ODD, S5_GROUPS, S5_STATE), 0.01)
    s5_log_dt = jax.random.uniform(ks[15], (N_ODD, S5_GROUPS), f32, math.log(0.001), math.log(0.1))
    s5_b_re = nrm(ks[16], (N_ODD, S5_GROUPS, S5_STATE, S5_GROUP), (2 * S5_GROUP) ** -0.5)
    s5_b_im = nrm(ks[17], (N_ODD, S5_GROUPS, S5_STATE, S5_GROUP), (2 * S5_GROUP) ** -0.5)
    s5_c_re = nrm(ks[18], (N_ODD, S5_GROUPS, S5_GROUP, S5_STATE), S5_STATE ** -0.5)
    s5_c_im = nrm(ks[19], (N_ODD, S5_GROUPS, S5_GROUP, S5_STATE), S5_STATE ** -0.5)
    s5_d = nrm(ks[20], (N_ODD, D_MODEL), 1.0)
    s5_w_glu = nrm(ks[21], (N_ODD, D_MODEL, 2 * D_MODEL), D_MODEL ** -0.5)
    return {'x': x, 'positions': positions,
            'ffn_norm': ffn_norm, 'ffn_w_gate': ffn_w_gate, 'ffn_w_up': ffn_w_up, 'ffn_w_down': ffn_w_down,
            'mix_norm': mix_norm,
            'ev_w_in': ev_w_in, 'ev_q_norm': ev_q_norm, 'ev_k_norm': ev_k_norm,
            'ev_pool_w': ev_pool_w, 'ev_pool_scale': ev_pool_scale, 'ev_w_out': ev_w_out,
            's5_a_re': s5_a_re, 's5_a_im': s5_a_im, 's5_log_dt': s5_log_dt,
            's5_b_re': s5_b_re, 's5_b_im': s5_b_im, 's5_c_re': s5_c_re, 's5_c_im': s5_c_im,
            's5_d': s5_d, 's5_w_glu': s5_w_glu}


def reference(x, positions, ffn_norm, ffn_w_gate, ffn_w_up, ffn_w_down, mix_norm,
              ev_w_in, ev_q_norm, ev_k_norm, ev_pool_w, ev_pool_scale, ev_w_out,
              s5_a_re, s5_a_im, s5_log_dt, s5_b_re, s5_b_im, s5_c_re, s5_c_im,
              s5_d, s5_w_glu):
    cos, sin = _rotary_tables(positions)
    for layer in range(DEPTH):
        x = x + 0.5 * _swiglu(_rmsnorm(x, ffn_norm[layer, 0]), ffn_w_gate[layer, 0],
                              ffn_w_up[layer, 0], ffn_w_down[layer, 0])
        h = _rmsnorm(x, mix_norm[layer])
        j = layer // 2
        if layer % 2 == 0:
            mixed = _even_mixer(h, cos, sin, ev_w_in[j], ev_q_norm[j], ev_k_norm[j],
                                ev_pool_w[j], ev_pool_scale[j], ev_w_out[j])
        else:
            mixed = _odd_mixer(h, s5_a_re[j], s5_a_im[j], s5_log_dt[j], s5_b_re[j], s5_b_im[j],
                               s5_c_re[j], s5_c_im[j], s5_d[j], s5_w_glu[j])
        x = x + mixed.astype(x.dtype)
        x = x + 0.5 * _swiglu(_rmsnorm(x, ffn_norm[layer, 1]), ffn_w_gate[layer, 1],
                              ffn_w_up[layer, 1], ffn_w_down[layer, 1])
    return x
```

```python
import functools
import math

import jax
import jax.numpy as jnp
from jax import lax
from jax.experimental import pallas as pl
from jax.experimental.pallas import tpu as pltpu

F32 = jnp.float32
BF16 = jnp.bfloat16

D_MODEL = 1024
D_ATTN = 512
HEAD_DIM = 64
ROT_DIM = 16
ROPE_THETA = 500000.0
DILATIONS = (1, 4, 16)
N_BACK = 128
D_POOL = 512
POOL_WINDOWS = (2, 4, 8, 16)
POOL_C = 128
S5_GROUP = 16
S5_GROUPS = 64
S5_STATE = 64
D_FF = 2816
EPS = 1e-6

LANES = 128
SUBLANES = 8
MXU_COLS = 256
VMEM_LIMIT = 56 * 1024 * 1024

NEG_BIG = -1e30


def _rms(x, g):
    return x * lax.rsqrt(jnp.mean(x * x, axis=-1, keepdims=True) + EPS) * g


def _const_spec(shape):
    nd = len(shape)
    return pl.BlockSpec(shape, lambda *_: (0,) * nd, pipeline_mode=pl.Buffered(1))


FFN_TM = 512


def _ffn_body(x_ref, g_ref, wg_ref, wu_ref, wd_ref, o_ref, a_ref):
    x = x_ref[...]
    h = _rms(x, g_ref[...]).astype(BF16)
    for c in range(D_FF // MXU_COLS):
        sl = slice(c * MXU_COLS, (c + 1) * MXU_COLS)
        gate = jnp.dot(h, wg_ref[:, sl], preferred_element_type=F32)
        up = jnp.dot(h, wu_ref[:, sl], preferred_element_type=F32)
        a_ref[:, sl] = (gate * jax.nn.sigmoid(gate) * up).astype(BF16)
    o_ref[...] = x + 0.5 * jnp.dot(a_ref[...], wd_ref[...], preferred_element_type=F32)


def _ffn(x2d, g, wg, wu, wd):
    t = x2d.shape[0]
    return pl.pallas_call(
        _ffn_body,
        grid=(t // FFN_TM,),
        in_specs=[
            pl.BlockSpec((FFN_TM, D_MODEL), lambda i: (i, 0)),
            _const_spec((1, D_MODEL)),
            _const_spec((D_MODEL, D_FF)),
            _const_spec((D_MODEL, D_FF)),
            _const_spec((D_FF, D_MODEL)),
        ],
        out_specs=pl.BlockSpec((FFN_TM, D_MODEL), lambda i: (i, 0)),
        out_shape=jax.ShapeDtypeStruct((t, D_MODEL), F32),
        scratch_shapes=[pltpu.VMEM((FFN_TM, D_FF), BF16)],
        compiler_params=pltpu.CompilerParams(
            dimension_semantics=("arbitrary",), vmem_limit_bytes=VMEM_LIMIT),
        name="ffn",
    )(x2d, g, wg, wu, wd)


EV_TM = 512
POOL_HALO = 16


def _head_norm_rot(t, gn, ones_bd, c128, s128):
    t2 = t * t
    hi = t2.astype(BF16)
    lo = (t2 - hi.astype(F32)).astype(BF16)
    ssq = (jnp.dot(hi, ones_bd, preferred_element_type=F32)
           + jnp.dot(lo, ones_bd, preferred_element_type=F32))
    y = t * lax.rsqrt(ssq * (1.0 / HEAD_DIM) + EPS) * gn
    lane = lax.broadcasted_iota(jnp.int32, (1, LANES), 1) % HEAD_DIM
    outs = []
    for j in range(D_ATTN // LANES):
        yj = y[:, j * LANES:(j + 1) * LANES]
        swapped = jnp.where(lane < ROT_DIM // 2,
                            pltpu.roll(yj, LANES - ROT_DIM // 2, 1),
                            pltpu.roll(yj, ROT_DIM // 2, 1))
        outs.append(yj * c128 + swapped * s128)
    return outs


def _evproj_body(x_ref, g_ref, w_ref, qn_ref, kn_ref, ones_ref, c_ref, s_ref,
                 pw_ref, ps_ref, q_ref, k_ref, v_ref, p_ref, carry_ref):
    it = pl.program_id(1)

    @pl.when(it == 0)
    def _():
        carry_ref[...] = jnp.zeros_like(carry_ref)

    x = x_ref[...]
    h = _rms(x, g_ref[...]).astype(BF16)
    proj = jnp.dot(h, w_ref[...], preferred_element_type=F32)
    ones_bd = ones_ref[...]
    c128 = c_ref[...]
    s128 = s_ref[...]
    qs = _head_norm_rot(proj[:, :D_ATTN], qn_ref[...], ones_bd, c128, s128)
    ks = _head_norm_rot(proj[:, D_ATTN:2 * D_ATTN], kn_ref[...], ones_bd, c128, s128)
    for j in range(D_ATTN // LANES):
        sl = slice(j * LANES, (j + 1) * LANES)
        q_ref[:, sl] = qs[j]
        k_ref[:, sl] = ks[j]
    v_ref[...] = proj[:, 2 * D_ATTN:3 * D_ATTN]

    p = proj[:, 3 * D_ATTN:]
    pe = jnp.concatenate([carry_ref[...], p], axis=0)
    carry_ref[...] = p[EV_TM - POOL_HALO:, :]
    pos1 = (it * EV_TM + 1 + lax.broadcasted_iota(jnp.int32, (EV_TM, 1), 0)).astype(F32)
    for g, w in enumerate(POOL_WINDOWS):
        sl = slice(g * POOL_C, (g + 1) * POOL_C)
        s = pe[:, sl]
        shift = 1
        while shift < w:
            s = s + pltpu.roll(s, shift, 0)
            shift *= 2
        count = jnp.minimum(pos1, float(w))
        pooled = s[POOL_HALO:, :] / count - p[:, sl]
        y = jnp.dot(pooled.astype(BF16), pw_ref[g], preferred_element_type=F32)
        p_ref[:, sl] = y * ps_ref[:, sl]


def _evproj(x, g, w_in, qn, kn, ones_bd, c128, s128, pool_w, pool_scale):
    b, s, _ = x.shape
    tok = lambda bi, i: (bi, i, 0)
    out = jax.ShapeDtypeStruct((b, s, D_ATTN), F32)
    return pl.pallas_call(
        _evproj_body,
        grid=(b, s // EV_TM),
        in_specs=[
            pl.BlockSpec((None, EV_TM, D_MODEL), tok),
            _const_spec((1, D_MODEL)),
            _const_spec((D_MODEL, 4 * D_ATTN)),
            _const_spec((1, D_ATTN)),
            _const_spec((1, D_ATTN)),
            _const_spec((D_ATTN, D_ATTN)),
            pl.BlockSpec((None, EV_TM, LANES), tok),
            pl.BlockSpec((None, EV_TM, LANES), tok),
            _const_spec((len(POOL_WINDOWS), POOL_C, POOL_C)),
            _const_spec((1, D_POOL)),
        ],
        out_specs=[pl.BlockSpec((None, EV_TM, D_ATTN), tok)] * 4,
        out_shape=[out] * 4,
        scratch_shapes=[pltpu.VMEM((POOL_HALO, D_POOL), F32)],
        compiler_params=pltpu.CompilerParams(
            dimension_semantics=("arbitrary", "arbitrary"), vmem_limit_bytes=VMEM_LIMIT),
        name="evproj",
    )(x, g, w_in, qn, kn, ones_bd, c128, s128, pool_w, pool_scale)


ATT_BLK = N_BACK


def _attn_body(q_ref, k_ref, v_ref, o_ref, m_ref, l_ref, acc_ref, *, seq):
    lane = lax.broadcasted_iota(jnp.int32, (1, LANES), 1)
    head0 = lane < HEAD_DIM
    qi = lax.broadcasted_iota(jnp.int32, (ATT_BLK, 2 * ATT_BLK), 0)
    kj = lax.broadcasted_iota(jnp.int32, (ATT_BLK, 2 * ATT_BLK), 1)
    rel = qi - kj
    units = seq // ATT_BLK

    def rows(start, n, d):
        return pl.ds(start, n) if d == 1 else pl.ds(start, n, stride=d)

    for d in DILATIONS:
        nblk = units // d

        def unit(u, carry, d=d, nblk=nblk):
            r = u // nblk
            nb = u % nblk
            kb = jnp.maximum(nb - 1, 0) * ATT_BLK
            qstart = r + d * ATT_BLK * nb
            kstart = r + d * kb
            off = nb * ATT_BLK - kb
            q2 = q_ref[rows(qstart, ATT_BLK, d), :] * (HEAD_DIM ** -0.5)
            k2 = k_ref[rows(kstart, 2 * ATT_BLK, d), :].astype(BF16)
            v2 = v_ref[rows(kstart, 2 * ATT_BLK, d), :].astype(BF16)
            delta = rel + off
            valid = (delta >= 0) & (delta <= N_BACK)
            ms, ls, os_ = [], [], []
            for hd in range(2):
                qh = jnp.where(head0 if hd == 0 else ~head0, q2, 0.0).astype(BF16)
                sc = lax.dot_general(qh, k2, (((1,), (1,)), ((), ())),
                                     preferred_element_type=F32)
                sc = jnp.where(valid, sc, NEG_BIG)
                m = jnp.max(sc, axis=-1, keepdims=True)
                p = jnp.exp(sc - m)
                ms.append(m)
                ls.append(jnp.sum(p, axis=-1, keepdims=True))
                os_.append(jnp.dot(p.astype(BF16), v2, preferred_element_type=F32))
            m_new = jnp.where(head0, ms[0], ms[1])
            l_new = jnp.where(head0, ls[0], ls[1])
            o_new = jnp.where(head0, os_[0], os_[1])
            qrows = rows(qstart, ATT_BLK, d)
            if d == DILATIONS[0]:
                m_ref[qrows, :] = m_new
                l_ref[qrows, :] = l_new
                acc_ref[qrows, :] = o_new
            else:
                m_old = m_ref[qrows, :]
                m_tot = jnp.maximum(m_old, m_new)
                a_old = jnp.exp(m_old - m_tot)
                a_new = jnp.exp(m_new - m_tot)
                m_ref[qrows, :] = m_tot
                l_ref[qrows, :] = l_ref[qrows, :] * a_old + l_new * a_new
                acc_ref[qrows, :] = acc_ref[qrows, :] * a_old + o_new * a_new
            return carry

        lax.fori_loop(0, units, unit, 0)

    def finish(i, carry):
        rs = pl.ds(pl.multiple_of(i * ATT_BLK, ATT_BLK), ATT_BLK)
        o_ref[rs, :] = acc_ref[rs, :] / l_ref[rs, :]
        return carry

    lax.fori_loop(0, units, finish, 0)


def _attn(q, k, v):
    b, s, _ = q.shape
    spec = pl.BlockSpec((None, s, LANES), lambda bi, hp: (bi, 0, hp))
    return pl.pallas_call(
        functools.partial(_attn_body, seq=s),
        grid=(b, D_ATTN // LANES),
        in_specs=[spec] * 3,
        out_specs=spec,
        out_shape=jax.ShapeDtypeStruct((b, s, D_ATTN), F32),
        scratch_shapes=[pltpu.VMEM((s, LANES), F32)] * 3,
        compiler_params=pltpu.CompilerParams(
            dimension_semantics=("arbitrary", "arbitrary"), vmem_limit_bytes=VMEM_LIMIT),
        name="dilated_attn",
    )(q, k, v)


OP_TM = 1024


def _oproj_body(x_ref, a_ref, p_ref, wa_ref, wp_ref, o_ref):
    o_ref[...] = (x_ref[...]
                  + jnp.dot(a_ref[...].astype(BF16), wa_ref[...], preferred_element_type=F32)
                  + jnp.dot(p_ref[...].astype(BF16), wp_ref[...], preferred_element_type=F32))


def _oproj(x2d, attn2d, pool2d, wa, wp):
    t = x2d.shape[0]
    row = lambda i: (i, 0)
    return pl.pallas_call(
        _oproj_body,
        grid=(t // OP_TM,),
        in_specs=[
            pl.BlockSpec((OP_TM, D_MODEL), row),
            pl.BlockSpec((OP_TM, D_ATTN), row),
            pl.BlockSpec((OP_TM, D_POOL), row),
            _const_spec((D_ATTN, D_MODEL)),
            _const_spec((D_POOL, D_MODEL)),
        ],
        out_specs=pl.BlockSpec((OP_TM, D_MODEL), row),
        out_shape=jax.ShapeDtypeStruct((t, D_MODEL), F32),
        compiler_params=pltpu.CompilerParams(
            dimension_semantics=("arbitrary",), vmem_limit_bytes=VMEM_LIMIT),
        name="oproj",
    )(x2d, attn2d, pool2d, wa, wp)


def _s5prep_body(are_ref, aim_ref, ldt_ref, brt_ref, bit_ref, lbr_ref, lbi_ref, bbr_ref, bbi_ref):
    lam_re = jnp.minimum(are_ref[...], -1e-4)
    lam_im = aim_ref[...]
    dt = jnp.exp(ldt_ref[...])
    mag = jnp.exp(lam_re * dt)
    lbr = mag * jnp.cos(lam_im * dt)
    lbi = mag * jnp.sin(lam_im * dt)
    den = lam_re * lam_re + lam_im * lam_im
    nre = lbr - 1.0
    cr = (nre * lam_re + lbi * lam_im) / den
    ci = (lbi * lam_re - nre * lam_im) / den
    lbr_ref[...] = lbr
    lbi_ref[...] = lbi
    br = brt_ref[...]
    bi = bit_ref[...]
    cr3 = cr[:, None, :]
    ci3 = ci[:, None, :]
    bbr_ref[...] = cr3 * br - ci3 * bi
    bbi_ref[...] = cr3 * bi + ci3 * br


def _s5prep(a_re, a_im, log_dt, b_re_t, b_im_t):
    gp = jax.ShapeDtypeStruct((S5_GROUPS, S5_STATE), F32)
    gcp = jax.ShapeDtypeStruct((S5_GROUPS, S5_GROUP, S5_STATE), F32)
    return pl.pallas_call(
        _s5prep_body,
        out_shape=[gp, gp, gcp, gcp],
        name="s5prep",
    )(a_re, a_im, log_dt, b_re_t, b_im_t)


S5_LC = 64
S5_SLABS = D_MODEL // LANES
S5_HALF = LANES // S5_GROUP * S5_STATE


def _gelu_tanh(y):
    return 0.5 * y * (1.0 + jnp.tanh(math.sqrt(2.0 / math.pi) * (y + 0.044715 * (y * y * y))))


def _s5_body(x_ref, g_ref, bb_ref, cc_ref, lam_ref, d_ref, wglu_ref, o_ref,
             utb_ref, bu_ref, st_ref, z_ref, gtb_ref, *, batch):
    rows = S5_LC * batch

    @pl.when(pl.program_id(0) == 0)
    def _():
        st_ref[...] = jnp.zeros_like(st_ref)

    for b in range(batch):
        hb = _rms(x_ref[b], g_ref[...])
        for j in range(S5_SLABS):
            utb_ref[j, pl.ds(b, S5_LC, stride=batch), :] = hb[:, j * LANES:(j + 1) * LANES]

    for j in range(S5_SLABS):
        bu_ref[j] = jnp.dot(utb_ref[j].astype(BF16), bb_ref[j], preferred_element_type=F32)

    for j in range(S5_SLABS):
        ar = lam_ref[j, :, :S5_HALF]
        ai = lam_ref[j, :, S5_HALF:]

        def step(t, h, j=j, ar=ar, ai=ai):
            hr, hi = h
            rs = pl.ds(pl.multiple_of(t * batch, batch), batch)
            nr = ar * hr - ai * hi + bu_ref[j, rs, :S5_HALF]
            ni = ar * hi + ai * hr + bu_ref[j, rs, S5_HALF:]
            bu_ref[j, rs, :S5_HALF] = nr
            bu_ref[j, rs, S5_HALF:] = ni
            return nr, ni

        hr, hi = lax.fori_loop(0, S5_LC, step,
                               (st_ref[j, :, :S5_HALF], st_ref[j, :, S5_HALF:]), unroll=4)
        st_ref[j, :, :S5_HALF] = hr
        st_ref[j, :, S5_HALF:] = hi

    for j in range(S5_SLABS):
        sl = slice(j * LANES, (j + 1) * LANES)
        y = jnp.dot(bu_ref[j].astype(BF16), cc_ref[j], preferred_element_type=F32)
        y = y + d_ref[:, sl] * utb_ref[j]
        z_ref[:, sl] = _gelu_tanh(y).astype(BF16)

    vg = jnp.dot(z_ref[...], wglu_ref[...], preferred_element_type=F32)
    glu = vg[:, :D_MODEL] * jax.nn.sigmoid(vg[:, D_MODEL:])
    for j in range(S5_SLABS):
        gtb_ref[j] = glu[:, j * LANES:(j + 1) * LANES]

    for b in range(batch):
        for j in range(S5_SLABS):
            sl = slice(j * LANES, (j + 1) * LANES)
            o_ref[b, :, sl] = x_ref[b, :, sl] + gtb_ref[j, pl.ds(b, S5_LC, stride=batch), :]


def _s5(x, g, bb, cc, lam, d_skip, w_glu):
    b, s, _ = x.shape
    assert b == SUBLANES
    rows = S5_LC * b
    blk = pl.BlockSpec((b, S5_LC, D_MODEL), lambda i: (0, i, 0))
    return pl.pallas_call(
        functools.partial(_s5_body, batch=b),
        grid=(s // S5_LC,),
        in_specs=[
            blk,
            _const_spec((1, D_MODEL)),
            _const_spec((S5_SLABS, LANES, 2 * S5_HALF)),
            _const_spec((S5_SLABS, 2 * S5_HALF, LANES)),
            _const_spec((S5_SLABS, SUBLANES, 2 * S5_HALF)),
            _const_spec((1, D_MODEL)),
            _const_spec((D_MODEL, 2 * D_MODEL)),
        ],
        out_specs=blk,
        out_shape=jax.ShapeDtypeStruct(x.shape, F32),
        scratch_shapes=[
            pltpu.VMEM((S5_SLABS, rows, LANES), F32),
            pltpu.VMEM((S5_SLABS, rows, 2 * S5_HALF), F32),
            pltpu.VMEM((S5_SLABS, SUBLANES, 2 * S5_HALF), F32),
            pltpu.VMEM((rows, D_MODEL), BF16),
            pltpu.VMEM((S5_SLABS, rows, LANES), F32),
        ],
        compiler_params=pltpu.CompilerParams(
            dimension_semantics=("arbitrary",), vmem_limit_bytes=VMEM_LIMIT),
        name="s5_glu",
    )(x, g, bb, cc, lam, d_skip, w_glu)


def _rotary_lane_tables(positions):
    half = ROT_DIM // 2
    inv_freq = ROPE_THETA ** (-jnp.arange(0, ROT_DIM, 2, dtype=F32) / ROT_DIM)
    ang = positions.astype(F32)[..., None] * inv_freq
    cos, sin = jnp.cos(ang), jnp.sin(ang)
    shp = cos.shape[:-1] + (HEAD_DIM - ROT_DIM,)
    c64 = jnp.concatenate([cos, cos, jnp.ones(shp, F32)], axis=-1)
    s64 = jnp.concatenate([-sin, sin, jnp.zeros(shp, F32)], axis=-1)
    return jnp.concatenate([c64, c64], axis=-1), jnp.concatenate([s64, s64], axis=-1)


def _s5_matrices(lbr, lbi, bbr, bbi, c_re, c_im):
    gl = LANES // S5_GROUP
    eye = jnp.eye(gl, dtype=F32)

    def in_blocks(m):
        m = m.reshape(S5_SLABS, gl, S5_GROUP, S5_STATE)
        return jnp.einsum('jgcp,gh->jgchp', m, eye).reshape(S5_SLABS, LANES, S5_HALF)

    def out_blocks(m):
        m = m.reshape(S5_SLABS, gl, S5_GROUP, S5_STATE)
        return jnp.einsum('jgcp,gh->jgphc', m, eye).reshape(S5_SLABS, S5_HALF, LANES)

    bb = jnp.concatenate([in_blocks(bbr), in_blocks(bbi)], axis=-1).astype(BF16)
    cc = jnp.concatenate([out_blocks(c_re), out_blocks(-c_im)], axis=1).astype(BF16)
    lam = jnp.concatenate([lbr.reshape(S5_SLABS, 1, S5_HALF), lbi.reshape(S5_SLABS, 1, S5_HALF)], axis=-1)
    lam = jnp.broadcast_to(lam, (S5_SLABS, SUBLANES, 2 * S5_HALF))
    return bb, cc, lam


def kernel(x, positions, ffn_norm, ffn_w_gate, ffn_w_up, ffn_w_down, mix_norm, ev_w_in, ev_q_norm, ev_k_norm, ev_pool_w, ev_pool_scale, ev_w_out, s5_a_re, s5_a_im, s5_log_dt, s5_b_re, s5_b_im, s5_c_re, s5_c_im, s5_d, s5_w_glu):
    b, s, d = x.shape
    t = b * s
    depth = ffn_norm.shape[0]
    c128, s128 = _rotary_lane_tables(positions)
    ones_bd = jnp.kron(jnp.eye(D_ATTN // HEAD_DIM, dtype=F32),
                       jnp.ones((HEAD_DIM, HEAD_DIM), F32)).astype(BF16)

    def ffn(xx, layer, half):
        return _ffn(xx.reshape(t, d), ffn_norm[layer, half][None, :],
                    ffn_w_gate[layer, half].astype(BF16), ffn_w_up[layer, half].astype(BF16),
                    ffn_w_down[layer, half].astype(BF16)).reshape(b, s, d)

    for layer in range(depth):
        x = ffn(x, layer, 0)
        j = layer // 2
        g = mix_norm[layer][None, :]
        if layer % 2 == 0:
            q, k, v, pool = _evproj(
                x, g, ev_w_in[j].astype(BF16),
                jnp.tile(ev_q_norm[j], D_ATTN // HEAD_DIM)[None, :],
                jnp.tile(ev_k_norm[j], D_ATTN // HEAD_DIM)[None, :],
                ones_bd, c128, s128, ev_pool_w[j].astype(BF16), ev_pool_scale[j][None, :])
            attn = _attn(q, k, v)
            w_out = ev_w_out[j].astype(BF16)
            x = _oproj(x.reshape(t, d), attn.reshape(t, D_ATTN), pool.reshape(t, D_POOL),
                       w_out[:D_ATTN], w_out[D_ATTN:]).reshape(b, s, d)
        else:
            lbr, lbi, bbr, bbi = _s5prep(
                s5_a_re[j], s5_a_im[j], s5_log_dt[j][:, None],
                s5_b_re[j].transpose(0, 2, 1), s5_b_im[j].transpose(0, 2, 1))
            bb, cc, lam = _s5_matrices(lbr, lbi, bbr, bbi, s5_c_re[j], s5_c_im[j])
            x = _s5(x, g, bb, cc, lam, s5_d[j][None, :], s5_w_glu[j].astype(BF16))
        x = ffn(x, layer, 1)
    return x
```

```python
import functools
import math

import jax
import jax.numpy as jnp
from jax import lax
from jax.experimental import pallas as pl
from jax.experimental.pallas import tpu as pltpu

F32 = jnp.float32
BF16 = jnp.bfloat16

D_MODEL = 1024
D_ATTN = 512
HEAD_DIM = 64
ROT_DIM = 16
ROPE_THETA = 500000.0
DILATIONS = (1, 4, 16)
N_BACK = 128
D_POOL = 512
POOL_WINDOWS = (2, 4, 8, 16)
POOL_C = 128
S5_GROUP = 16
S5_GROUPS = 64
S5_STATE = 64
D_FF = 2816
EPS = 1e-6

LANES = 128
SUBLANES = 8
MXU_COLS = 256
VMEM_LIMIT = 56 * 1024 * 1024

NEG_BIG = -1e30


def _rms(x, g):
    return x * lax.rsqrt(jnp.mean(x * x, axis=-1, keepdims=True) + EPS) * g


def _const_spec(shape):
    nd = len(shape)
    return pl.BlockSpec(shape, lambda *_: (0,) * nd, pipeline_mode=pl.Buffered(1))


FFN_TM = 512


def _ffn_body(x_ref, g_ref, wg_ref, wu_ref, wd_ref, o_ref, a_ref):
    x = x_ref[...]
    h = _rms(x, g_ref[...]).astype(BF16)
    for c in range(D_FF // MXU_COLS):
        sl = slice(c * MXU_COLS, (c + 1) * MXU_COLS)
        gate = jnp.dot(h, wg_ref[:, sl], preferred_element_type=F32)
        up = jnp.dot(h, wu_ref[:, sl], preferred_element_type=F32)
        a_ref[:, sl] = (gate * jax.nn.sigmoid(gate) * up).astype(BF16)
    o_ref[...] = x + 0.5 * jnp.dot(a_ref[...], wd_ref[...], preferred_element_type=F32)


def _ffn(x2d, g, wg, wu, wd):
    t = x2d.shape[0]
    return pl.pallas_call(
        _ffn_body,
        grid=(t // FFN_TM,),
        in_specs=[
            pl.BlockSpec((FFN_TM, D_MODEL), lambda i: (i, 0)),
            _const_spec((1, D_MODEL)),
            _const_spec((D_MODEL, D_FF)),
            _const_spec((D_MODEL, D_FF)),
            _const_spec((D_FF, D_MODEL)),
        ],
        out_specs=pl.BlockSpec((FFN_TM, D_MODEL), lambda i: (i, 0)),
        out_shape=jax.ShapeDtypeStruct((t, D_MODEL), F32),
        scratch_shapes=[pltpu.VMEM((FFN_TM, D_FF), BF16)],
        compiler_params=pltpu.CompilerParams(
            dimension_semantics=("arbitrary",), vmem_limit_bytes=VMEM_LIMIT),
        name="ffn",
    )(x2d, g, wg, wu, wd)


EV_TM = 512
POOL_HALO = 16


def _head_norm_rot(t, gn, ones_bd, c128, s128):
    t2 = t * t
    hi = t2.astype(BF16)
    lo = (t2 - hi.astype(F32)).astype(BF16)
    ssq = (jnp.dot(hi, ones_bd, preferred_element_type=F32)
           + jnp.dot(lo, ones_bd, preferred_element_type=F32))
    y = t * lax.rsqrt(ssq * (1.0 / HEAD_DIM) + EPS) * gn
    lane = lax.broadcasted_iota(jnp.int32, (1, LANES), 1) % HEAD_DIM
    outs = []
    for j in range(D_ATTN // LANES):
        yj = y[:, j * LANES:(j + 1) * LANES]
        swapped = jnp.where(lane < ROT_DIM // 2,
                            pltpu.roll(yj, LANES - ROT_DIM // 2, 1),
                            pltpu.roll(yj, ROT_DIM // 2, 1))
        outs.append(yj * c128 + swapped * s128)
    return outs


def _evproj_body(x_ref, g_ref, w_ref, qn_ref, kn_ref, ones_ref, c_ref, s_ref,
                 pw_ref, ps_ref, q_ref, k_ref, v_ref, p_ref, carry_ref):
    it = pl.program_id(1)

    @pl.when(it == 0)
    def _():
        carry_ref[...] = jnp.zeros_like(carry_ref)

    x = x_ref[...]
    h = _rms(x, g_ref[...]).astype(BF16)
    proj = jnp.dot(h, w_ref[...], preferred_element_type=F32)
    ones_bd = ones_ref[...]
    c128 = c_ref[...]
    s128 = s_ref[...]
    qs = _head_norm_rot(proj[:, :D_ATTN], qn_ref[...], ones_bd, c128, s128)
    ks = _head_norm_rot(proj[:, D_ATTN:2 * D_ATTN], kn_ref[...], ones_bd, c128, s128)
    for j in range(D_ATTN // LANES):
        sl = slice(j * LANES, (j + 1) * LANES)
        q_ref[:, sl] = qs[j]
        k_ref[:, sl] = ks[j]
    v_ref[...] = proj[:, 2 * D_ATTN:3 * D_ATTN]

    p = proj[:, 3 * D_ATTN:]
    pe = jnp.concatenate([carry_ref[...], p], axis=0)
    carry_ref[...] = p[EV_TM - POOL_HALO:, :]
    pos1 = (it * EV_TM + 1 + lax.broadcasted_iota(jnp.int32, (EV_TM, 1), 0)).astype(F32)
    for g, w in enumerate(POOL_WINDOWS):
        sl = slice(g * POOL_C, (g + 1) * POOL_C)
        s = pe[:, sl]
        shift = 1
        while shift < w:
            s = s + pltpu.roll(s, shift, 0)
            shift *= 2
        count = jnp.minimum(pos1, float(w))
        pooled = s[POOL_HALO:, :] / count - p[:, sl]
        y = jnp.dot(pooled.astype(BF16), pw_ref[g], preferred_element_type=F32)
        p_ref[:, sl] = y * ps_ref[:, sl]


def _evproj(x, g, w_in, qn, kn, ones_bd, c128, s128, pool_w, pool_scale):
    b, s, _ = x.shape
    tok = lambda bi, i: (bi, i, 0)
    out = jax.ShapeDtypeStruct((b, s, D_ATTN), F32)
    return pl.pallas_call(
        _evproj_body,
        grid=(b, s // EV_TM),
        in_specs=[
            pl.BlockSpec((None, EV_TM, D_MODEL), tok),
            _const_spec((1, D_MODEL)),
            _const_spec((D_MODEL, 4 * D_ATTN)),
            _const_spec((1, D_ATTN)),
            _const_spec((1, D_ATTN)),
            _const_spec((D_ATTN, D_ATTN)),
            pl.BlockSpec((None, EV_TM, LANES), tok),
            pl.BlockSpec((None, EV_TM, LANES), tok),
            _const_spec((len(POOL_WINDOWS), POOL_C, POOL_C)),
            _const_spec((1, D_POOL)),
        ],
        out_specs=[pl.BlockSpec((None, EV_TM, D_ATTN), tok)] * 4,
        out_shape=[out] * 4,
        scratch_shapes=[pltpu.VMEM((POOL_HALO, D_POOL), F32)],
        compiler_params=pltpu.CompilerParams(
            dimension_semantics=("arbitrary", "arbitrary"), vmem_limit_bytes=VMEM_LIMIT),
        name="evproj",
    )(x, g, w_in, qn, kn, ones_bd, c128, s128, pool_w, pool_scale)


ATT_BLK = N_BACK


ATT_UNROLL = 8
ATT_QSCALE = HEAD_DIM ** -0.5 * math.log2(math.e)


def _attn_body(q_ref, k_ref, v_ref, o_ref, bias_ref, *stats, seq):
    lane = lax.broadcasted_iota(jnp.int32, (1, LANES), 1)
    head0 = lane < HEAD_DIM
    units = seq // ATT_BLK

    qi = lax.broadcasted_iota(jnp.int32, (ATT_BLK, 2 * ATT_BLK), 0)
    kj = lax.broadcasted_iota(jnp.int32, (ATT_BLK, 2 * ATT_BLK), 1)
    for idx, off in enumerate((0, ATT_BLK)):
        delta = qi - kj + off
        bias_ref[idx] = jnp.where((delta >= 0) & (delta <= N_BACK), 0.0, NEG_BIG)

    def rows(start, n, d):
        return pl.ds(start, n) if d == 1 else pl.ds(start, n, stride=d)

    ones = jnp.ones((2 * ATT_BLK, LANES), BF16)

    for pi, d in enumerate(DILATIONS):
        nblk = units // d
        num_ref, den_ref, max_ref = stats[3 * pi:3 * pi + 3]

        def unit(u, d=d, nblk=nblk, num_ref=num_ref, den_ref=den_ref, max_ref=max_ref):
            r = u // nblk
            nb = u % nblk
            kb = jnp.maximum(nb - 1, 0) * ATT_BLK
            qrows = rows(r + d * ATT_BLK * nb, ATT_BLK, d)
            krows = rows(r + d * kb, 2 * ATT_BLK, d)
            q2 = q_ref[qrows, :] * ATT_QSCALE
            qq = jnp.concatenate([jnp.where(head0, q2, 0.0), jnp.where(head0, 0.0, q2)],
                                 axis=0).astype(BF16)
            k2 = k_ref[krows, :].astype(BF16)
            vaug = jnp.concatenate([v_ref[krows, :].astype(BF16), ones], axis=1)
            sc = lax.dot_general(qq, k2, (((1,), (1,)), ((), ())), preferred_element_type=F32)
            bias = bias_ref[jnp.minimum(nb, 1)]
            sc = sc + jnp.concatenate([bias, bias], axis=0)
            m = jnp.max(sc, axis=-1, keepdims=True)
            p = jnp.exp2(sc - m).astype(BF16)
            res = jnp.dot(p, vaug, preferred_element_type=F32)
            num_ref[qrows, :] = jnp.where(head0, res[:ATT_BLK, :LANES], res[ATT_BLK:, :LANES])
            den_ref[qrows, :] = jnp.where(head0, res[:ATT_BLK, LANES:], res[ATT_BLK:, LANES:])
            max_ref[qrows, :] = jnp.where(head0, m[:ATT_BLK], m[ATT_BLK:])

        def group(it, carry, unit=unit):
            for c in range(ATT_UNROLL):
                unit(it * ATT_UNROLL + c)
            return carry

        lax.fori_loop(0, units // ATT_UNROLL, group, 0)

    def finish(i, carry):
        rs = pl.ds(pl.multiple_of(i * ATT_BLK, ATT_BLK), ATT_BLK)
        ms = [stats[3 * pi + 2][rs, :] for pi in range(len(DILATIONS))]
        m_tot = functools.reduce(jnp.maximum, ms)
        ws = [jnp.exp2(m - m_tot) for m in ms]
        num = sum(w * stats[3 * pi][rs, :] for pi, w in enumerate(ws))
        den = sum(w * stats[3 * pi + 1][rs, :] for pi, w in enumerate(ws))
        o_ref[rs, :] = num / den
        return carry

    lax.fori_loop(0, units, finish, 0)


def _attn(q, k, v):
    b, s, _ = q.shape
    spec = pl.BlockSpec((None, s, LANES), lambda bi, hp: (bi, 0, hp))
    return pl.pallas_call(
        functools.partial(_attn_body, seq=s),
        grid=(b, D_ATTN // LANES),
        in_specs=[spec] * 3,
        out_specs=spec,
        out_shape=jax.ShapeDtypeStruct((b, s, D_ATTN), F32),
        scratch_shapes=([pltpu.VMEM((2, ATT_BLK, 2 * ATT_BLK), F32)]
                        + [pltpu.VMEM((s, LANES), F32)] * (3 * len(DILATIONS))),
        compiler_params=pltpu.CompilerParams(
            dimension_semantics=("arbitrary", "arbitrary"), vmem_limit_bytes=VMEM_LIMIT),
        name="dilated_attn",
    )(q, k, v)


OP_TM = 1024


def _oproj_body(x_ref, a_ref, p_ref, wa_ref, wp_ref, o_ref):
    o_ref[...] = (x_ref[...]
                  + jnp.dot(a_ref[...].astype(BF16), wa_ref[...], preferred_element_type=F32)
                  + jnp.dot(p_ref[...].astype(BF16), wp_ref[...], preferred_element_type=F32))


def _oproj(x2d, attn2d, pool2d, wa, wp):
    t = x2d.shape[0]
    row = lambda i: (i, 0)
    return pl.pallas_call(
        _oproj_body,
        grid=(t // OP_TM,),
        in_specs=[
            pl.BlockSpec((OP_TM, D_MODEL), row),
            pl.BlockSpec((OP_TM, D_ATTN), row),
            pl.BlockSpec((OP_TM, D_POOL), row),
            _const_spec((D_ATTN, D_MODEL)),
            _const_spec((D_POOL, D_MODEL)),
        ],
        out_specs=pl.BlockSpec((OP_TM, D_MODEL), row),
        out_shape=jax.ShapeDtypeStruct((t, D_MODEL), F32),
        compiler_params=pltpu.CompilerParams(
            dimension_semantics=("arbitrary",), vmem_limit_bytes=VMEM_LIMIT),
        name="oproj",
    )(x2d, attn2d, pool2d, wa, wp)


def _s5prep_body(are_ref, aim_ref, ldt_ref, brt_ref, bit_ref, lbr_ref, lbi_ref, bbr_ref, bbi_ref):
    lam_re = jnp.minimum(are_ref[...], -1e-4)
    lam_im = aim_ref[...]
    dt = jnp.exp(ldt_ref[...])
    mag = jnp.exp(lam_re * dt)
    lbr = mag * jnp.cos(lam_im * dt)
    lbi = mag * jnp.sin(lam_im * dt)
    den = lam_re * lam_re + lam_im * lam_im
    nre = lbr - 1.0
    cr = (nre * lam_re + lbi * lam_im) / den
    ci = (lbi * lam_re - nre * lam_im) / den
    lbr_ref[...] = lbr
    lbi_ref[...] = lbi
    br = brt_ref[...]
    bi = bit_ref[...]
    cr3 = cr[:, None, :]
    ci3 = ci[:, None, :]
    bbr_ref[...] = cr3 * br - ci3 * bi
    bbi_ref[...] = cr3 * bi + ci3 * br


def _s5prep(a_re, a_im, log_dt, b_re_t, b_im_t):
    gp = jax.ShapeDtypeStruct((S5_GROUPS, S5_STATE), F32)
    gcp = jax.ShapeDtypeStruct((S5_GROUPS, S5_GROUP, S5_STATE), F32)
    return pl.pallas_call(
        _s5prep_body,
        out_shape=[gp, gp, gcp, gcp],
        name="s5prep",
    )(a_re, a_im, log_dt, b_re_t, b_im_t)


S5_LC = 64
S5_SLABS = D_MODEL // LANES
S5_HALF = LANES // S5_GROUP * S5_STATE


def _gelu_tanh(y):
    return 0.5 * y * (1.0 + jnp.tanh(math.sqrt(2.0 / math.pi) * (y + 0.044715 * (y * y * y))))


def _s5_body(x_ref, g_ref, bb_ref, cc_ref, lam_ref, d_ref, wglu_ref, o_ref,
             utb_ref, bu_ref, st_ref, z_ref, gtb_ref, *, batch):
    rows = S5_LC * batch

    @pl.when(pl.program_id(0) == 0)
    def _():
        st_ref[...] = jnp.zeros_like(st_ref)

    for b in range(batch):
        hb = _rms(x_ref[b], g_ref[...])
        for j in range(S5_SLABS):
            utb_ref[j, pl.ds(b, S5_LC, stride=batch), :] = hb[:, j * LANES:(j + 1) * LANES]

    for j in range(S5_SLABS):
        bu_ref[j] = jnp.dot(utb_ref[j].astype(BF16), bb_ref[j], preferred_element_type=F32)

    for j in range(S5_SLABS):
        ar = lam_ref[j, :, :S5_HALF]
        ai = lam_ref[j, :, S5_HALF:]

        def step(t, h, j=j, ar=ar, ai=ai):
            hr, hi = h
            rs = pl.ds(pl.multiple_of(t * batch, batch), batch)
            nr = ar * hr - ai * hi + bu_ref[j, rs, :S5_HALF]
            ni = ar * hi + ai * hr + bu_ref[j, rs, S5_HALF:]
            bu_ref[j, rs, :S5_HALF] = nr
            bu_ref[j, rs, S5_HALF:] = ni
            return nr, ni

        hr, hi = lax.fori_loop(0, S5_LC, step,
                               (st_ref[j, :, :S5_HALF], st_ref[j, :, S5_HALF:]), unroll=4)
        st_ref[j, :, :S5_HALF] = hr
        st_ref[j, :, S5_HALF:] = hi

    for j in range(S5_SLABS):
        sl = slice(j * LANES, (j + 1) * LANES)
        y = jnp.dot(bu_ref[j].astype(BF16), cc_ref[j], preferred_element_type=F32)
        y = y + d_ref[:, sl] * utb_ref[j]
        z_ref[:, sl] = _gelu_tanh(y).astype(BF16)

    vg = jnp.dot(z_ref[...], wglu_ref[...], preferred_element_type=F32)
    glu = vg[:, :D_MODEL] * jax.nn.sigmoid(vg[:, D_MODEL:])
    for j in range(S5_SLABS):
        gtb_ref[j] = glu[:, j * LANES:(j + 1) * LANES]

    for b in range(batch):
        for j in range(S5_SLABS):
            sl = slice(j * LANES, (j + 1) * LANES)
            o_ref[b, :, sl] = x_ref[b, :, sl] + gtb_ref[j, pl.ds(b, S5_LC, stride=batch), :]


def _s5(x, g, bb, cc, lam, d_skip, w_glu):
    b, s, _ = x.shape
    assert b == SUBLANES
    rows = S5_LC * b
    blk = pl.BlockSpec((b, S5_LC, D_MODEL), lambda i: (0, i, 0))
    return pl.pallas_call(
        functools.partial(_s5_body, batch=b),
        grid=(s // S5_LC,),
        in_specs=[
            blk,
            _const_spec((1, D_MODEL)),
            _const_spec((S5_SLABS, LANES, 2 * S5_HALF)),
            _const_spec((S5_SLABS, 2 * S5_HALF, LANES)),
            _const_spec((S5_SLABS, SUBLANES, 2 * S5_HALF)),
            _const_spec((1, D_MODEL)),
            _const_spec((D_MODEL, 2 * D_MODEL)),
        ],
        out_specs=blk,
        out_shape=jax.ShapeDtypeStruct(x.shape, F32),
        scratch_shapes=[
            pltpu.VMEM((S5_SLABS, rows, LANES), F32),
            pltpu.VMEM((S5_SLABS, rows, 2 * S5_HALF), F32),
            pltpu.VMEM((S5_SLABS, SUBLANES, 2 * S5_HALF), F32),
            pltpu.VMEM((rows, D_MODEL), BF16),
            pltpu.VMEM((S5_SLABS, rows, LANES), F32),
        ],
        compiler_params=pltpu.CompilerParams(
            dimension_semantics=("arbitrary",), vmem_limit_bytes=VMEM_LIMIT),
        name="s5_glu",
    )(x, g, bb, cc, lam, d_skip, w_glu)


def _rotary_lane_tables(positions):
    half = ROT_DIM // 2
    inv_freq = ROPE_THETA ** (-jnp.arange(0, ROT_DIM, 2, dtype=F32) / ROT_DIM)
    ang = positions.astype(F32)[..., None] * inv_freq
    cos, sin = jnp.cos(ang), jnp.sin(ang)
    shp = cos.shape[:-1] + (HEAD_DIM - ROT_DIM,)
    c64 = jnp.concatenate([cos, cos, jnp.ones(shp, F32)], axis=-1)
    s64 = jnp.concatenate([-sin, sin, jnp.zeros(shp, F32)], axis=-1)
    return jnp.concatenate([c64, c64], axis=-1), jnp.concatenate([s64, s64], axis=-1)


def _s5_matrices(lbr, lbi, bbr, bbi, c_re, c_im):
    gl = LANES // S5_GROUP
    eye = jnp.eye(gl, dtype=F32)

    def in_blocks(m):
        m = m.reshape(S5_SLABS, gl, S5_GROUP, S5_STATE)
        return jnp.einsum('jgcp,gh->jgchp', m, eye).reshape(S5_SLABS, LANES, S5_HALF)

    def out_blocks(m):
        m = m.reshape(S5_SLABS, gl, S5_GROUP, S5_STATE)
        return jnp.einsum('jgcp,gh->jgphc', m, eye).reshape(S5_SLABS, S5_HALF, LANES)

    bb = jnp.concatenate([in_blocks(bbr), in_blocks(bbi)], axis=-1).astype(BF16)
    cc = jnp.concatenate([out_blocks(c_re), out_blocks(-c_im)], axis=1).astype(BF16)
    lam = jnp.concatenate([lbr.reshape(S5_SLABS, 1, S5_HALF), lbi.reshape(S5_SLABS, 1, S5_HALF)], axis=-1)
    lam = jnp.broadcast_to(lam, (S5_SLABS, SUBLANES, 2 * S5_HALF))
    return bb, cc, lam


def kernel(x, positions, ffn_norm, ffn_w_gate, ffn_w_up, ffn_w_down, mix_norm, ev_w_in, ev_q_norm, ev_k_norm, ev_pool_w, ev_pool_scale, ev_w_out, s5_a_re, s5_a_im, s5_log_dt, s5_b_re, s5_b_im, s5_c_re, s5_c_im, s5_d, s5_w_glu):
    b, s, d = x.shape
    t = b * s
    depth = ffn_norm.shape[0]
    c128, s128 = _rotary_lane_tables(positions)
    ones_bd = jnp.kron(jnp.eye(D_ATTN // HEAD_DIM, dtype=F32),
                       jnp.ones((HEAD_DIM, HEAD_DIM), F32)).astype(BF16)

    def ffn(xx, layer, half):
        return _ffn(xx.reshape(t, d), ffn_norm[layer, half][None, :],
                    ffn_w_gate[layer, half].astype(BF16), ffn_w_up[layer, half].astype(BF16),
                    ffn_w_down[layer, half].astype(BF16)).reshape(b, s, d)

    for layer in range(depth):
        x = ffn(x, layer, 0)
        j = layer // 2
        g = mix_norm[layer][None, :]
        if layer % 2 == 0:
            q, k, v, pool = _evproj(
                x, g, ev_w_in[j].astype(BF16),
                jnp.tile(ev_q_norm[j], D_ATTN // HEAD_DIM)[None, :],
                jnp.tile(ev_k_norm[j], D_ATTN // HEAD_DIM)[None, :],
                ones_bd, c128, s128, ev_pool_w[j].astype(BF16), ev_pool_scale[j][None, :])
            attn = _attn(q, k, v)
            w_out = ev_w_out[j].astype(BF16)
            x = _oproj(x.reshape(t, d), attn.reshape(t, D_ATTN), pool.reshape(t, D_POOL),
                       w_out[:D_ATTN], w_out[D_ATTN:]).reshape(b, s, d)
        else:
            lbr, lbi, bbr, bbi = _s5prep(
                s5_a_re[j], s5_a_im[j], s5_log_dt[j][:, None],
                s5_b_re[j].transpose(0, 2, 1), s5_b_im[j].transpose(0, 2, 1))
            bb, cc, lam = _s5_matrices(lbr, lbi, bbr, bbi, s5_c_re[j], s5_c_im[j])
            x = _s5(x, g, bb, cc, lam, s5_d[j][None, :], s5_w_glu[j].astype(BF16))
        x = ffn(x, layer, 1)
    return x
```

```python
import functools
import math

import jax
import jax.numpy as jnp
from jax import lax
from jax.experimental import pallas as pl
from jax.experimental.pallas import tpu as pltpu

F32 = jnp.float32
BF16 = jnp.bfloat16

D_MODEL = 1024
D_ATTN = 512
HEAD_DIM = 64
ROT_DIM = 16
ROPE_THETA = 500000.0
DILATIONS = (1, 4, 16)
N_BACK = 128
D_POOL = 512
POOL_WINDOWS = (2, 4, 8, 16)
POOL_C = 128
S5_GROUP = 16
S5_GROUPS = 64
S5_STATE = 64
D_FF = 2816
EPS = 1e-6

LANES = 128
SUBLANES = 8
MXU_COLS = 256
VMEM_LIMIT = 56 * 1024 * 1024

NEG_BIG = -1e30


def _rms(x, g):
    return x * lax.rsqrt(jnp.mean(x * x, axis=-1, keepdims=True) + EPS) * g


def _const_spec(shape):
    nd = len(shape)
    return pl.BlockSpec(shape, lambda *_: (0,) * nd, pipeline_mode=pl.Buffered(1))


FFN_TM = 512


def _ffn_body(x_ref, g_ref, wg_ref, wu_ref, wd_ref, o_ref, a_ref):
    x = x_ref[...]
    h = _rms(x, g_ref[...]).astype(BF16)
    for c in range(D_FF // MXU_COLS):
        sl = slice(c * MXU_COLS, (c + 1) * MXU_COLS)
        gate = jnp.dot(h, wg_ref[:, sl], preferred_element_type=F32)
        up = jnp.dot(h, wu_ref[:, sl], preferred_element_type=F32)
        a_ref[:, sl] = (gate * jax.nn.sigmoid(gate) * up).astype(BF16)
    o_ref[...] = x + 0.5 * jnp.dot(a_ref[...], wd_ref[...], preferred_element_type=F32)


def _ffn(x2d, g, wg, wu, wd, layer, half):
    t = x2d.shape[0]
    pick = lambda i: (layer, half, 0, 0)
    single = pl.Buffered(1)
    return pl.pallas_call(
        _ffn_body,
        grid=(t // FFN_TM,),
        in_specs=[
            pl.BlockSpec((FFN_TM, D_MODEL), lambda i: (i, 0)),
            _const_spec((1, D_MODEL)),
            pl.BlockSpec((None, None, D_MODEL, D_FF), pick, pipeline_mode=single),
            pl.BlockSpec((None, None, D_MODEL, D_FF), pick, pipeline_mode=single),
            pl.BlockSpec((None, None, D_FF, D_MODEL), pick, pipeline_mode=single),
        ],
        out_specs=pl.BlockSpec((FFN_TM, D_MODEL), lambda i: (i, 0)),
        out_shape=jax.ShapeDtypeStruct((t, D_MODEL), F32),
        scratch_shapes=[pltpu.VMEM((FFN_TM, D_FF), BF16)],
        compiler_params=pltpu.CompilerParams(
            dimension_semantics=("arbitrary",), vmem_limit_bytes=VMEM_LIMIT),
        name="ffn",
    )(x2d, g, wg, wu, wd)


EV_TM = 512
POOL_HALO = 16


def _head_norm_rot(t, gn, ones_bd, c128, s128):
    ssq = jnp.dot((t * t).astype(BF16), ones_bd, preferred_element_type=F32)
    y = t * lax.rsqrt(ssq * (1.0 / HEAD_DIM) + EPS) * gn
    lane = lax.broadcasted_iota(jnp.int32, (1, LANES), 1) % HEAD_DIM
    outs = []
    for j in range(D_ATTN // LANES):
        yj = y[:, j * LANES:(j + 1) * LANES]
        swapped = jnp.where(lane < ROT_DIM // 2,
                            pltpu.roll(yj, LANES - ROT_DIM // 2, 1),
                            pltpu.roll(yj, ROT_DIM // 2, 1))
        outs.append(yj * c128 + swapped * s128)
    return outs


def _evproj_body(x_ref, g_ref, w_ref, qn_ref, kn_ref, ones_ref, cs_ref, e_ref,
                 pw_ref, ps_ref, q_ref, k_ref, v_ref, p_ref, carry_ref):
    it = pl.program_id(1)

    @pl.when(it == 0)
    def _():
        carry_ref[...] = jnp.zeros_like(carry_ref)

    x = x_ref[...]
    h = _rms(x, g_ref[...]).astype(BF16)
    proj = jnp.dot(h, w_ref[...], preferred_element_type=F32)
    ones_bd = ones_ref[...]
    cs = cs_ref[...]
    cs_hi = cs.astype(BF16)
    cs_lo = (cs - cs_hi.astype(F32)).astype(BF16)
    spread = (jnp.dot(cs_hi, e_ref[...], preferred_element_type=F32)
              + jnp.dot(cs_lo, e_ref[...], preferred_element_type=F32))
    lane = lax.broadcasted_iota(jnp.int32, (1, LANES), 1) % HEAD_DIM
    c128 = spread[:, :LANES] + jnp.where(lane < ROT_DIM, 0.0, 1.0)
    s128 = spread[:, LANES:]
    qs = _head_norm_rot(proj[:, :D_ATTN], qn_ref[...], ones_bd, c128, s128)
    ks = _head_norm_rot(proj[:, D_ATTN:2 * D_ATTN], kn_ref[...], ones_bd, c128, s128)
    for j in range(D_ATTN // LANES):
        sl = slice(j * LANES, (j + 1) * LANES)
        q_ref[:, sl] = qs[j]
        k_ref[:, sl] = ks[j]
    v_ref[...] = proj[:, 2 * D_ATTN:3 * D_ATTN]

    p = proj[:, 3 * D_ATTN:]
    pe = jnp.concatenate([carry_ref[...], p], axis=0)
    carry_ref[...] = p[EV_TM - POOL_HALO:, :]
    pos1 = (it * EV_TM + 1 + lax.broadcasted_iota(jnp.int32, (EV_TM, 1), 0)).astype(F32)
    for g, w in enumerate(POOL_WINDOWS):
        sl = slice(g * POOL_C, (g + 1) * POOL_C)
        s = pe[:, sl]
        shift = 1
        while shift < w:
            s = s + pltpu.roll(s, shift, 0)
            shift *= 2
        count = jnp.minimum(pos1, float(w))
        pooled = s[POOL_HALO:, :] / count - p[:, sl]
        y = jnp.dot(pooled.astype(BF16), pw_ref[g], preferred_element_type=F32)
        p_ref[:, sl] = y * ps_ref[:, sl]


def _evproj(x, g, w_in, j, qn, kn, ones_bd, cs, spread, pool_w, pool_scale):
    b, s, _ = x.shape
    tok = lambda bi, i: (bi, i, 0)
    out = jax.ShapeDtypeStruct((b, s, D_ATTN), F32)
    single = pl.Buffered(1)
    return pl.pallas_call(
        _evproj_body,
        grid=(b, s // EV_TM),
        in_specs=[
            pl.BlockSpec((None, EV_TM, D_MODEL), tok),
            _const_spec((1, D_MODEL)),
            pl.BlockSpec((None, D_MODEL, 4 * D_ATTN), lambda bi, i: (j, 0, 0), pipeline_mode=single),
            _const_spec((1, D_ATTN)),
            _const_spec((1, D_ATTN)),
            _const_spec((D_ATTN, D_ATTN)),
            pl.BlockSpec((None, EV_TM, ROT_DIM), tok),
            _const_spec((ROT_DIM, 2 * LANES)),
            pl.BlockSpec((None, len(POOL_WINDOWS), POOL_C, POOL_C), lambda bi, i: (j, 0, 0, 0),
                         pipeline_mode=single),
            _const_spec((1, D_POOL)),
        ],
        out_specs=[pl.BlockSpec((None, EV_TM, D_ATTN), tok)] * 4,
        out_shape=[out] * 4,
        scratch_shapes=[pltpu.VMEM((POOL_HALO, D_POOL), F32)],
        compiler_params=pltpu.CompilerParams(
            dimension_semantics=("arbitrary", "arbitrary"), vmem_limit_bytes=VMEM_LIMIT),
        name="evproj",
    )(x, g, w_in, qn, kn, ones_bd, cs, spread, pool_w, pool_scale)


ATT_BLK = N_BACK


ATT_UNROLL = 8
ATT_QSCALE = HEAD_DIM ** -0.5 * math.log2(math.e)


def _attn_body(q_ref, k_ref, v_ref, o_ref, bias_ref, *stats, seq):
    lane = lax.broadcasted_iota(jnp.int32, (1, LANES), 1)
    head0 = lane < HEAD_DIM
    units = seq // ATT_BLK

    qi = lax.broadcasted_iota(jnp.int32, (ATT_BLK, 2 * ATT_BLK), 0)
    kj = lax.broadcasted_iota(jnp.int32, (ATT_BLK, 2 * ATT_BLK), 1)
    for idx, off in enumerate((0, ATT_BLK)):
        delta = qi - kj + off
        bias_ref[idx] = jnp.where((delta >= 0) & (delta <= N_BACK), 0.0, NEG_BIG)

    def rows(start, n, d):
        return pl.ds(start, n) if d == 1 else pl.ds(start, n, stride=d)

    ones = jnp.ones((2 * ATT_BLK, LANES), BF16)

    for pi, d in enumerate(DILATIONS):
        nblk = units // d
        num_ref, den_ref, max_ref = stats[3 * pi:3 * pi + 3]

        def unit(u, d=d, nblk=nblk, num_ref=num_ref, den_ref=den_ref, max_ref=max_ref):
            r = u // nblk
            nb = u % nblk
            kb = jnp.maximum(nb - 1, 0) * ATT_BLK
            qrows = rows(r + d * ATT_BLK * nb, ATT_BLK, d)
            krows = rows(r + d * kb, 2 * ATT_BLK, d)
            q2 = q_ref[qrows, :] * ATT_QSCALE
            qq = jnp.concatenate([jnp.where(head0, q2, 0.0), jnp.where(head0, 0.0, q2)],
                                 axis=0).astype(BF16)
            k2 = k_ref[krows, :].astype(BF16)
            vaug = jnp.concatenate([v_ref[krows, :].astype(BF16), ones], axis=1)
            sc = lax.dot_general(qq, k2, (((1,), (1,)), ((), ())), preferred_element_type=F32)
            bias = bias_ref[jnp.minimum(nb, 1)]
            sc = sc + jnp.concatenate([bias, bias], axis=0)
            m = jnp.max(sc, axis=-1, keepdims=True)
            p = jnp.exp2(sc - m).astype(BF16)
            res = jnp.dot(p, vaug, preferred_element_type=F32)
            num_ref[qrows, :] = jnp.where(head0, res[:ATT_BLK, :LANES], res[ATT_BLK:, :LANES])
            den_ref[qrows, :] = jnp.where(head0, res[:ATT_BLK, LANES:], res[ATT_BLK:, LANES:])
            max_ref[qrows, :] = jnp.where(head0, m[:ATT_BLK], m[ATT_BLK:])

        def group(it, carry, unit=unit):
            for c in range(ATT_UNROLL):
                unit(it * ATT_UNROLL + c)
            return carry

        lax.fori_loop(0, units // ATT_UNROLL, group, 0)

    def finish(i, carry):
        rs = pl.ds(pl.multiple_of(i * ATT_BLK, ATT_BLK), ATT_BLK)
        ms = [stats[3 * pi + 2][rs, :] for pi in range(len(DILATIONS))]
        m_tot = functools.reduce(jnp.maximum, ms)
        ws = [jnp.exp2(m - m_tot) for m in ms]
        num = sum(w * stats[3 * pi][rs, :] for pi, w in enumerate(ws))
        den = sum(w * stats[3 * pi + 1][rs, :] for pi, w in enumerate(ws))
        o_ref[rs, :] = num / den
        return carry

    lax.fori_loop(0, units, finish, 0)


def _attn(q, k, v):
    b, s, _ = q.shape
    spec = pl.BlockSpec((None, s, LANES), lambda bi, hp: (bi, 0, hp))
    return pl.pallas_call(
        functools.partial(_attn_body, seq=s),
        grid=(b, D_ATTN // LANES),
        in_specs=[spec] * 3,
        out_specs=spec,
        out_shape=jax.ShapeDtypeStruct((b, s, D_ATTN), F32),
        scratch_shapes=([pltpu.VMEM((2, ATT_BLK, 2 * ATT_BLK), F32)]
                        + [pltpu.VMEM((s, LANES), F32)] * (3 * len(DILATIONS))),
        compiler_params=pltpu.CompilerParams(
            dimension_semantics=("arbitrary", "arbitrary"), vmem_limit_bytes=VMEM_LIMIT),
        name="dilated_attn",
    )(q, k, v)


OP_TM = 1024


def _oproj_body(x_ref, a_ref, p_ref, wa_ref, wp_ref, o_ref):
    o_ref[...] = (x_ref[...]
                  + jnp.dot(a_ref[...].astype(BF16), wa_ref[...], preferred_element_type=F32)
                  + jnp.dot(p_ref[...].astype(BF16), wp_ref[...], preferred_element_type=F32))


def _oproj(x2d, attn2d, pool2d, w_out, j):
    t = x2d.shape[0]
    row = lambda i: (i, 0)
    single = pl.Buffered(1)
    return pl.pallas_call(
        _oproj_body,
        grid=(t // OP_TM,),
        in_specs=[
            pl.BlockSpec((OP_TM, D_MODEL), row),
            pl.BlockSpec((OP_TM, D_ATTN), row),
            pl.BlockSpec((OP_TM, D_POOL), row),
            pl.BlockSpec((None, D_ATTN, D_MODEL), lambda i: (j, 0, 0), pipeline_mode=single),
            pl.BlockSpec((None, D_POOL, D_MODEL), lambda i: (j, 1, 0), pipeline_mode=single),
        ],
        out_specs=pl.BlockSpec((OP_TM, D_MODEL), row),
        out_shape=jax.ShapeDtypeStruct((t, D_MODEL), F32),
        compiler_params=pltpu.CompilerParams(
            dimension_semantics=("arbitrary",), vmem_limit_bytes=VMEM_LIMIT),
        name="oproj",
    )(x2d, attn2d, pool2d, w_out, w_out)


def _s5prep_body(are_ref, aim_ref, ldt_ref, brt_ref, bit_ref, lbr_ref, lbi_ref, bbr_ref, bbi_ref):
    lam_re = jnp.minimum(are_ref[...], -1e-4)
    lam_im = aim_ref[...]
    dt = jnp.exp(ldt_ref[...])
    mag = jnp.exp(lam_re * dt)
    lbr = mag * jnp.cos(lam_im * dt)
    lbi = mag * jnp.sin(lam_im * dt)
    den = lam_re * lam_re + lam_im * lam_im
    nre = lbr - 1.0
    cr = (nre * lam_re + lbi * lam_im) / den
    ci = (lbi * lam_re - nre * lam_im) / den
    lbr_ref[...] = lbr
    lbi_ref[...] = lbi
    br = brt_ref[...]
    bi = bit_ref[...]
    cr3 = cr[:, None, :]
    ci3 = ci[:, None, :]
    bbr_ref[...] = cr3 * br - ci3 * bi
    bbi_ref[...] = cr3 * bi + ci3 * br


def _s5prep(a_re, a_im, log_dt, b_re_t, b_im_t):
    gp = jax.ShapeDtypeStruct((S5_GROUPS, S5_STATE), F32)
    gcp = jax.ShapeDtypeStruct((S5_GROUPS, S5_GROUP, S5_STATE), F32)
    return pl.pallas_call(
        _s5prep_body,
        out_shape=[gp, gp, gcp, gcp],
        name="s5prep",
    )(a_re, a_im, log_dt, b_re_t, b_im_t)


S5_LC = 64
S5_SLABS = D_MODEL // LANES
S5_HALF = LANES // S5_GROUP * S5_STATE


def _gelu_tanh(y):
    return 0.5 * y * (1.0 + jnp.tanh(math.sqrt(2.0 / math.pi) * (y + 0.044715 * (y * y * y))))


def _s5_body(x_ref, g_ref, bb_ref, cc_ref, lam_ref, d_ref, wglu_ref, o_ref,
             utb_ref, bu_ref, st_ref, z_ref, gtb_ref, *, batch):
    rows = S5_LC * batch

    @pl.when(pl.program_id(0) == 0)
    def _():
        st_ref[...] = jnp.zeros_like(st_ref)

    for b in range(batch):
        hb = _rms(x_ref[b], g_ref[...])
        for j in range(S5_SLABS):
            utb_ref[j, pl.ds(b, S5_LC, stride=batch), :] = hb[:, j * LANES:(j + 1) * LANES]

    for j in range(S5_SLABS):
        bu_ref[j] = jnp.dot(utb_ref[j].astype(BF16), bb_ref[j], preferred_element_type=F32)

    for j in range(S5_SLABS):
        ar = lam_ref[j, :, :S5_HALF]
        ai = lam_ref[j, :, S5_HALF:]

        def step(t, h, j=j, ar=ar, ai=ai):
            hr, hi = h
            rs = pl.ds(t * batch, batch)
            nr = ar * hr - ai * hi + bu_ref[j, rs, :S5_HALF]
            ni = ar * hi + ai * hr + bu_ref[j, rs, S5_HALF:]
            bu_ref[j, rs, :S5_HALF] = nr
            bu_ref[j, rs, S5_HALF:] = ni
            return nr, ni

        h = (st_ref[j, :, :S5_HALF], st_ref[j, :, S5_HALF:])
        for t in range(S5_LC):
            h = step(t, h)
        hr, hi = h
        st_ref[j, :, :S5_HALF] = hr
        st_ref[j, :, S5_HALF:] = hi

    for j in range(S5_SLABS):
        sl = slice(j * LANES, (j + 1) * LANES)
        y = jnp.dot(bu_ref[j].astype(BF16), cc_ref[j], preferred_element_type=F32)
        y = y + d_ref[:, sl] * utb_ref[j]
        z_ref[:, sl] = _gelu_tanh(y).astype(BF16)

    vg = jnp.dot(z_ref[...], wglu_ref[...], preferred_element_type=F32)
    glu = vg[:, :D_MODEL] * jax.nn.sigmoid(vg[:, D_MODEL:])
    for j in range(S5_SLABS):
        gtb_ref[j] = glu[:, j * LANES:(j + 1) * LANES]

    for b in range(batch):
        for j in range(S5_SLABS):
            sl = slice(j * LANES, (j + 1) * LANES)
            o_ref[b, :, sl] = x_ref[b, :, sl] + gtb_ref[j, pl.ds(b, S5_LC, stride=batch), :]


def _s5(x, g, bb, cc, lam, d_skip, w_glu):
    b, s, _ = x.shape
    assert b == SUBLANES
    rows = S5_LC * b
    blk = pl.BlockSpec((b, S5_LC, D_MODEL), lambda i: (0, i, 0))
    return pl.pallas_call(
        functools.partial(_s5_body, batch=b),
        grid=(s // S5_LC,),
        in_specs=[
            blk,
            _const_spec((1, D_MODEL)),
            _const_spec((S5_SLABS, LANES, 2 * S5_HALF)),
            _const_spec((S5_SLABS, 2 * S5_HALF, LANES)),
            _const_spec((S5_SLABS, SUBLANES, 2 * S5_HALF)),
            _const_spec((1, D_MODEL)),
            _const_spec((D_MODEL, 2 * D_MODEL)),
        ],
        out_specs=blk,
        out_shape=jax.ShapeDtypeStruct(x.shape, F32),
        scratch_shapes=[
            pltpu.VMEM((S5_SLABS, rows, LANES), F32),
            pltpu.VMEM((S5_SLABS, rows, 2 * S5_HALF), F32),
            pltpu.VMEM((S5_SLABS, SUBLANES, 2 * S5_HALF), F32),
            pltpu.VMEM((rows, D_MODEL), BF16),
            pltpu.VMEM((S5_SLABS, rows, LANES), F32),
        ],
        compiler_params=pltpu.CompilerParams(
            dimension_semantics=("arbitrary",), vmem_limit_bytes=VMEM_LIMIT),
        name="s5_glu",
    )(x, g, bb, cc, lam, d_skip, w_glu)


def _rotary_tables(positions):
    half = ROT_DIM // 2
    inv_freq = ROPE_THETA ** (-jnp.arange(0, ROT_DIM, 2, dtype=F32) / ROT_DIM)
    ang = inv_freq[:, None, None] * positions.astype(F32)[None]
    cs = jnp.concatenate([jnp.cos(ang), jnp.sin(ang)], axis=0).transpose(1, 2, 0)
    lane = jnp.arange(LANES) % HEAD_DIM
    freq = jnp.arange(half)[:, None]
    first = (lane[None, :] == freq).astype(F32)
    second = (lane[None, :] == freq + half).astype(F32)
    zeros = jnp.zeros_like(first)
    spread = jnp.concatenate([
        jnp.concatenate([first + second, zeros], axis=1),
        jnp.concatenate([zeros, second - first], axis=1),
    ], axis=0).astype(BF16)
    return cs, spread


def _s5_matrices(lbr, lbi, bbr, bbi, c_re, c_im):
    gl = LANES // S5_GROUP
    eye = jnp.eye(gl, dtype=F32)

    def in_blocks(m):
        m = m.reshape(S5_SLABS, gl, S5_GROUP, S5_STATE)
        return jnp.einsum('jgcp,gh->jgchp', m, eye).reshape(S5_SLABS, LANES, S5_HALF)

    def out_blocks(m):
        m = m.reshape(S5_SLABS, gl, S5_GROUP, S5_STATE)
        return jnp.einsum('jgcp,gh->jgphc', m, eye).reshape(S5_SLABS, S5_HALF, LANES)

    bb = jnp.concatenate([in_blocks(bbr), in_blocks(bbi)], axis=-1).astype(BF16)
    cc = jnp.concatenate([out_blocks(c_re), out_blocks(-c_im)], axis=1).astype(BF16)
    lam = jnp.concatenate([lbr.reshape(S5_SLABS, 1, S5_HALF), lbi.reshape(S5_SLABS, 1, S5_HALF)], axis=-1)
    lam = jnp.broadcast_to(lam, (S5_SLABS, SUBLANES, 2 * S5_HALF))
    return bb, cc, lam


def kernel(x, positions, ffn_norm, ffn_w_gate, ffn_w_up, ffn_w_down, mix_norm, ev_w_in, ev_q_norm, ev_k_norm, ev_pool_w, ev_pool_scale, ev_w_out, s5_a_re, s5_a_im, s5_log_dt, s5_b_re, s5_b_im, s5_c_re, s5_c_im, s5_d, s5_w_glu):
    b, s, d = x.shape
    t = b * s
    depth = ffn_norm.shape[0]
    cs, spread = _rotary_tables(positions)
    ones_bd = jnp.kron(jnp.eye(D_ATTN // HEAD_DIM, dtype=F32),
                       jnp.ones((HEAD_DIM, HEAD_DIM), F32)).astype(BF16)
    wg, wu, wd = ffn_w_gate.astype(BF16), ffn_w_up.astype(BF16), ffn_w_down.astype(BF16)
    w_in, pool_w, w_out = ev_w_in.astype(BF16), ev_pool_w.astype(BF16), ev_w_out.astype(BF16)

    def ffn(xx, layer, half):
        return _ffn(xx.reshape(t, d), ffn_norm[layer, half][None, :],
                    wg, wu, wd, layer, half).reshape(b, s, d)

    for layer in range(depth):
        x = ffn(x, layer, 0)
        j = layer // 2
        g = mix_norm[layer][None, :]
        if layer % 2 == 0:
            q, k, v, pool = _evproj(
                x, g, w_in, j,
                jnp.tile(ev_q_norm[j], D_ATTN // HEAD_DIM)[None, :],
                jnp.tile(ev_k_norm[j], D_ATTN // HEAD_DIM)[None, :],
                ones_bd, cs, spread, pool_w, ev_pool_scale[j][None, :])
            attn = _attn(q, k, v)
            x = _oproj(x.reshape(t, d), attn.reshape(t, D_ATTN), pool.reshape(t, D_POOL),
                       w_out, j).reshape(b, s, d)
        else:
            lbr, lbi, bbr, bbi = _s5prep(
                s5_a_re[j], s5_a_im[j], s5_log_dt[j][:, None],
                s5_b_re[j].transpose(0, 2, 1), s5_b_im[j].transpose(0, 2, 1))
            bb, cc, lam = _s5_matrices(lbr, lbi, bbr, bbi, s5_c_re[j], s5_c_im[j])
            x = _s5(x, g, bb, cc, lam, s5_d[j][None, :], s5_w_glu[j].astype(BF16))
        x = ffn(x, layer, 1)
    return x
```

```python
import functools
import math

import jax
import jax.numpy as jnp
from jax import lax
from jax.experimental import pallas as pl
from jax.experimental.pallas import tpu as pltpu

F32 = jnp.float32
BF16 = jnp.bfloat16

D_MODEL = 1024
D_ATTN = 512
HEAD_DIM = 64
ROT_DIM = 16
ROPE_THETA = 500000.0
DILATIONS = (1, 4, 16)
N_BACK = 128
D_POOL = 512
POOL_WINDOWS = (2, 4, 8, 16)
POOL_C = 128
S5_GROUP = 16
S5_GROUPS = 64
S5_STATE = 64
D_FF = 2816
EPS = 1e-6

LANES = 128
SUBLANES = 8
MXU_COLS = 256
VMEM_LIMIT = 56 * 1024 * 1024

NEG_BIG = -1e30


def _rms(x, g):
    return x * lax.rsqrt(jnp.mean(x * x, axis=-1, keepdims=True) + EPS) * g


def _const_spec(shape):
    nd = len(shape)
    return pl.BlockSpec(shape, lambda *_: (0,) * nd, pipeline_mode=pl.Buffered(1))


FFN_TM = 512


def _ffn_body(x_ref, g_ref, wg_ref, wu_ref, wd_ref, o_ref, a_ref):
    x = x_ref[...]
    h = _rms(x, g_ref[...]).astype(BF16)
    for c in range(D_FF // MXU_COLS):
        sl = slice(c * MXU_COLS, (c + 1) * MXU_COLS)
        gate = jnp.dot(h, wg_ref[:, sl], preferred_element_type=F32)
        up = jnp.dot(h, wu_ref[:, sl], preferred_element_type=F32)
        a_ref[:, sl] = (gate * jax.nn.sigmoid(gate) * up).astype(BF16)
    o_ref[...] = x + 0.5 * jnp.dot(a_ref[...], wd_ref[...], preferred_element_type=F32)


def _ffn(x2d, g, wg, wu, wd, layer, half):
    t = x2d.shape[0]
    pick = lambda i: (layer, half, 0, 0)
    single = pl.Buffered(1)
    return pl.pallas_call(
        _ffn_body,
        grid=(t // FFN_TM,),
        in_specs=[
            pl.BlockSpec((FFN_TM, D_MODEL), lambda i: (i, 0)),
            _const_spec((1, D_MODEL)),
            pl.BlockSpec((None, None, D_MODEL, D_FF), pick, pipeline_mode=single),
            pl.BlockSpec((None, None, D_MODEL, D_FF), pick, pipeline_mode=single),
            pl.BlockSpec((None, None, D_FF, D_MODEL), pick, pipeline_mode=single),
        ],
        out_specs=pl.BlockSpec((FFN_TM, D_MODEL), lambda i: (i, 0)),
        out_shape=jax.ShapeDtypeStruct((t, D_MODEL), F32),
        scratch_shapes=[pltpu.VMEM((FFN_TM, D_FF), BF16)],
        compiler_params=pltpu.CompilerParams(
            dimension_semantics=("arbitrary",), vmem_limit_bytes=VMEM_LIMIT),
        name="ffn",
    )(x2d, g, wg, wu, wd)


EV_TM = 512
POOL_HALO = 16
ATT_RES = 4


def _head_norm_rot(t, gn, ones_bd, c128, s128):
    ssq = jnp.dot((t * t).astype(BF16), ones_bd, preferred_element_type=F32)
    y = t * lax.rsqrt(ssq * (1.0 / HEAD_DIM) + EPS) * gn
    lane = lax.broadcasted_iota(jnp.int32, (1, LANES), 1) % HEAD_DIM
    outs = []
    for j in range(D_ATTN // LANES):
        yj = y[:, j * LANES:(j + 1) * LANES]
        swapped = jnp.where(lane < ROT_DIM // 2,
                            pltpu.roll(yj, LANES - ROT_DIM // 2, 1),
                            pltpu.roll(yj, ROT_DIM // 2, 1))
        outs.append(yj * c128 + swapped * s128)
    return outs


def _evproj_body(x_ref, g_ref, w_ref, qn_ref, kn_ref, ones_ref, cs_ref, e_ref,
                 pw_ref, ps_ref, q_ref, k_ref, v_ref, p_ref, carry_ref, stage_ref):
    it = pl.program_id(1)

    @pl.when(it == 0)
    def _():
        carry_ref[...] = jnp.zeros_like(carry_ref)

    x = x_ref[...]
    h = _rms(x, g_ref[...]).astype(BF16)
    proj = jnp.dot(h, w_ref[...], preferred_element_type=F32)
    ones_bd = ones_ref[...]
    cs = cs_ref[...]
    cs_hi = cs.astype(BF16)
    cs_lo = (cs - cs_hi.astype(F32)).astype(BF16)
    spread = (jnp.dot(cs_hi, e_ref[...], preferred_element_type=F32)
              + jnp.dot(cs_lo, e_ref[...], preferred_element_type=F32))
    lane = lax.broadcasted_iota(jnp.int32, (1, LANES), 1) % HEAD_DIM
    c128 = spread[:, :LANES] + jnp.where(lane < ROT_DIM, 0.0, 1.0)
    s128 = spread[:, LANES:]
    qs = _head_norm_rot(proj[:, :D_ATTN], qn_ref[...], ones_bd, c128, s128)
    ks = _head_norm_rot(proj[:, D_ATTN:2 * D_ATTN], kn_ref[...], ones_bd, c128, s128)
    sub = EV_TM // ATT_RES
    for a, (o_ref, slabs) in enumerate((
            (q_ref, qs), (k_ref, ks),
            (v_ref, [proj[:, 2 * D_ATTN + j * LANES:2 * D_ATTN + (j + 1) * LANES]
                     for j in range(D_ATTN // LANES)]))):
        for j in range(D_ATTN // LANES):
            stage_ref[a * (D_ATTN // LANES) + j] = slabs[j]
        for j in range(D_ATTN // LANES):
            for r in range(ATT_RES):
                o_ref[r, :, j * LANES:(j + 1) * LANES] = stage_ref[
                    a * (D_ATTN // LANES) + j, pl.ds(r, sub, stride=ATT_RES), :]

    p = proj[:, 3 * D_ATTN:]
    pe = jnp.concatenate([carry_ref[...], p], axis=0)
    carry_ref[...] = p[EV_TM - POOL_HALO:, :]
    pos1 = (it * EV_TM + 1 + lax.broadcasted_iota(jnp.int32, (EV_TM, 1), 0)).astype(F32)
    for g, w in enumerate(POOL_WINDOWS):
        sl = slice(g * POOL_C, (g + 1) * POOL_C)
        s = pe[:, sl]
        shift = 1
        while shift < w:
            s = s + pltpu.roll(s, shift, 0)
            shift *= 2
        count = jnp.minimum(pos1, float(w))
        pooled = s[POOL_HALO:, :] / count - p[:, sl]
        y = jnp.dot(pooled.astype(BF16), pw_ref[g], preferred_element_type=F32)
        p_ref[:, sl] = y * ps_ref[:, sl]


def _evproj(x, g, w_in, j, qn, kn, ones_bd, cs, spread, pool_w, pool_scale):
    b, s, _ = x.shape
    tok = lambda bi, i: (bi, i, 0)
    out = jax.ShapeDtypeStruct((b, s, D_ATTN), F32)
    res_out = jax.ShapeDtypeStruct((b, ATT_RES, s // ATT_RES, D_ATTN), F32)
    res_spec = pl.BlockSpec((None, ATT_RES, EV_TM // ATT_RES, D_ATTN), lambda bi, i: (bi, 0, i, 0))
    single = pl.Buffered(1)
    return pl.pallas_call(
        _evproj_body,
        grid=(b, s // EV_TM),
        in_specs=[
            pl.BlockSpec((None, EV_TM, D_MODEL), tok),
            _const_spec((1, D_MODEL)),
            pl.BlockSpec((None, D_MODEL, 4 * D_ATTN), lambda bi, i: (j, 0, 0), pipeline_mode=single),
            _const_spec((1, D_ATTN)),
            _const_spec((1, D_ATTN)),
            _const_spec((D_ATTN, D_ATTN)),
            pl.BlockSpec((None, EV_TM, ROT_DIM), tok),
            _const_spec((ROT_DIM, 2 * LANES)),
            pl.BlockSpec((None, len(POOL_WINDOWS), POOL_C, POOL_C), lambda bi, i: (j, 0, 0, 0),
                         pipeline_mode=single),
            _const_spec((1, D_POOL)),
        ],
        out_specs=[res_spec] * 3 + [pl.BlockSpec((None, EV_TM, D_ATTN), tok)],
        out_shape=[res_out] * 3 + [out],
        scratch_shapes=[pltpu.VMEM((POOL_HALO, D_POOL), F32),
                        pltpu.VMEM((3 * D_ATTN // LANES, EV_TM, LANES), F32)],
        compiler_params=pltpu.CompilerParams(
            dimension_semantics=("arbitrary", "arbitrary"), vmem_limit_bytes=VMEM_LIMIT),
        name="evproj",
    )(x, g, w_in, qn, kn, ones_bd, cs, spread, pool_w, pool_scale)


ATT_BLK = N_BACK


ATT_UNROLL = 8
ATT_QSCALE = HEAD_DIM ** -0.5 * math.log2(math.e)


def _attn_body(q_ref, k_ref, v_ref, o_ref, bias_ref, *stats, seq):
    lane = lax.broadcasted_iota(jnp.int32, (1, LANES), 1)
    head0 = lane < HEAD_DIM
    units = seq // ATT_BLK
    sub = seq // ATT_RES
    qc = ATT_BLK // ATT_RES
    kc = 2 * qc

    qi = lax.broadcasted_iota(jnp.int32, (ATT_BLK, 2 * ATT_BLK), 0)
    kj = lax.broadcasted_iota(jnp.int32, (ATT_BLK, 2 * ATT_BLK), 1)
    qpos = (qi % qc) * ATT_RES + qi // qc
    kpos = (kj % kc) * ATT_RES + kj // kc
    for idx, (dq, off) in enumerate(((qi - kj, 0), (qi - kj, ATT_BLK),
                                     (qpos - kpos, 0), (qpos - kpos, ATT_BLK))):
        delta = dq + off
        bias_ref[idx] = jnp.where((delta >= 0) & (delta <= N_BACK), 0.0, NEG_BIG)

    ones = jnp.ones((2 * ATT_BLK, LANES), BF16)

    def attend(q2, k2, v2, bias):
        q2 = q2 * ATT_QSCALE
        qq = jnp.concatenate([jnp.where(head0, q2, 0.0), jnp.where(head0, 0.0, q2)],
                             axis=0).astype(BF16)
        vaug = jnp.concatenate([v2.astype(BF16), ones], axis=1)
        sc = lax.dot_general(qq, k2.astype(BF16), (((1,), (1,)), ((), ())),
                             preferred_element_type=F32)
        sc = sc + jnp.concatenate([bias, bias], axis=0)
        m = jnp.max(sc, axis=-1, keepdims=True)
        p = jnp.exp2(sc - m).astype(BF16)
        res = jnp.dot(p, vaug, preferred_element_type=F32)
        return (jnp.where(head0, res[:ATT_BLK, :LANES], res[ATT_BLK:, :LANES]),
                jnp.where(head0, res[:ATT_BLK, LANES:], res[ATT_BLK:, LANES:]),
                jnp.where(head0, m[:ATT_BLK], m[ATT_BLK:]))

    def unit_consecutive(nb, refs):
        q0 = pl.multiple_of(nb * qc, qc)
        k0 = pl.multiple_of(jnp.maximum(nb - 1, 0) * qc, qc)
        q2 = jnp.concatenate([q_ref[c, pl.ds(q0, qc), :] for c in range(ATT_RES)], axis=0)
        k2 = jnp.concatenate([k_ref[c, pl.ds(k0, kc), :] for c in range(ATT_RES)], axis=0)
        v2 = jnp.concatenate([v_ref[c, pl.ds(k0, kc), :] for c in range(ATT_RES)], axis=0)
        outs = attend(q2, k2, v2, bias_ref[2 + jnp.minimum(nb, 1)])
        for ref, val in zip(refs, outs):
            for c in range(ATT_RES):
                ref[c, pl.ds(q0, qc), :] = val[c * qc:(c + 1) * qc]

    def unit_strided(u, refs, step):
        per_res = units // ATT_RES
        r = u // per_res
        w = u % per_res
        nblk = per_res // step
        rr = w // nblk
        nb = w % nblk
        kb = jnp.maximum(nb - 1, 0)

        def rows(blk, n):
            start = rr + step * ATT_BLK * blk
            return pl.ds(start, n) if step == 1 else pl.ds(start, n, stride=step)

        qrows = rows(nb, ATT_BLK)
        krows = rows(kb, 2 * ATT_BLK)
        outs = attend(q_ref[r, qrows, :], k_ref[r, krows, :], v_ref[r, krows, :],
                      bias_ref[jnp.minimum(nb, 1)])
        for ref, val in zip(refs, outs):
            ref[r, qrows, :] = val

    for pi, d in enumerate(DILATIONS):
        refs = stats[3 * pi:3 * pi + 3]

        def group(it, carry, d=d, refs=refs):
            for c in range(ATT_UNROLL):
                u = it * ATT_UNROLL + c
                if d == 1:
                    unit_consecutive(u, refs)
                else:
                    unit_strided(u, refs, d // ATT_RES)
            return carry

        lax.fori_loop(0, units // ATT_UNROLL, group, 0)

    def finish(i, carry):
        per_res = sub // ATT_BLK
        r = i // per_res
        rs = pl.ds(pl.multiple_of((i % per_res) * ATT_BLK, ATT_BLK), ATT_BLK)
        ms = [stats[3 * pi + 2][r, rs, :] for pi in range(len(DILATIONS))]
        m_tot = functools.reduce(jnp.maximum, ms)
        ws = [jnp.exp2(m - m_tot) for m in ms]
        num = sum(w * stats[3 * pi][r, rs, :] for pi, w in enumerate(ws))
        den = sum(w * stats[3 * pi + 1][r, rs, :] for pi, w in enumerate(ws))
        o_ref[r, rs, :] = num / den
        return carry

    lax.fori_loop(0, units, finish, 0)


def _attn(q, k, v):
    b, res, sub, _ = q.shape
    spec = pl.BlockSpec((None, res, sub, LANES), lambda bi, hp: (bi, 0, 0, hp))
    return pl.pallas_call(
        functools.partial(_attn_body, seq=res * sub),
        grid=(b, D_ATTN // LANES),
        in_specs=[spec] * 3,
        out_specs=spec,
        out_shape=jax.ShapeDtypeStruct(q.shape, F32),
        scratch_shapes=([pltpu.VMEM((4, ATT_BLK, 2 * ATT_BLK), F32)]
                        + [pltpu.VMEM((res, sub, LANES), F32)] * (3 * len(DILATIONS))),
        compiler_params=pltpu.CompilerParams(
            dimension_semantics=("arbitrary", "arbitrary"), vmem_limit_bytes=VMEM_LIMIT),
        name="dilated_attn",
    )(q, k, v)


OP_TM = 1024


def _oproj_body(x_ref, a_ref, p_ref, wa_ref, wp_ref, o_ref, nat_ref):
    sub = OP_TM // ATT_RES
    for j in range(D_ATTN // LANES):
        for r in range(ATT_RES):
            nat_ref[j, pl.ds(r, sub, stride=ATT_RES), :] = a_ref[r, :, j * LANES:(j + 1) * LANES]
    attn = jnp.concatenate([nat_ref[j] for j in range(D_ATTN // LANES)], axis=1)
    o_ref[...] = (x_ref[...]
                  + jnp.dot(attn.astype(BF16), wa_ref[...], preferred_element_type=F32)
                  + jnp.dot(p_ref[...].astype(BF16), wp_ref[...], preferred_element_type=F32))


def _oproj(x, attn, pool, w_out, j):
    b, s, _ = x.shape
    tok = lambda bi, i: (bi, i, 0)
    single = pl.Buffered(1)
    return pl.pallas_call(
        _oproj_body,
        grid=(b, s // OP_TM),
        in_specs=[
            pl.BlockSpec((None, OP_TM, D_MODEL), tok),
            pl.BlockSpec((None, ATT_RES, OP_TM // ATT_RES, D_ATTN), lambda bi, i: (bi, 0, i, 0)),
            pl.BlockSpec((None, OP_TM, D_POOL), tok),
            pl.BlockSpec((None, D_ATTN, D_MODEL), lambda bi, i: (j, 0, 0), pipeline_mode=single),
            pl.BlockSpec((None, D_POOL, D_MODEL), lambda bi, i: (j, 1, 0), pipeline_mode=single),
        ],
        out_specs=pl.BlockSpec((None, OP_TM, D_MODEL), tok),
        out_shape=jax.ShapeDtypeStruct(x.shape, F32),
        scratch_shapes=[pltpu.VMEM((D_ATTN // LANES, OP_TM, LANES), F32)],
        compiler_params=pltpu.CompilerParams(
            dimension_semantics=("arbitrary", "arbitrary"), vmem_limit_bytes=VMEM_LIMIT),
        name="oproj",
    )(x, attn, pool, w_out, w_out)


def _s5prep_body(are_ref, aim_ref, ldt_ref, brt_ref, bit_ref, lbr_ref, lbi_ref, bbr_ref, bbi_ref):
    lam_re = jnp.minimum(are_ref[...], -1e-4)
    lam_im = aim_ref[...]
    dt = jnp.exp(ldt_ref[...])
    mag = jnp.exp(lam_re * dt)
    lbr = mag * jnp.cos(lam_im * dt)
    lbi = mag * jnp.sin(lam_im * dt)
    den = lam_re * lam_re + lam_im * lam_im
    nre = lbr - 1.0
    cr = (nre * lam_re + lbi * lam_im) / den
    ci = (lbi * lam_re - nre * lam_im) / den
    lbr_ref[...] = lbr
    lbi_ref[...] = lbi
    br = brt_ref[...]
    bi = bit_ref[...]
    cr3 = cr[:, None, :]
    ci3 = ci[:, None, :]
    bbr_ref[...] = cr3 * br - ci3 * bi
    bbi_ref[...] = cr3 * bi + ci3 * br


def _s5prep(a_re, a_im, log_dt, b_re_t, b_im_t):
    gp = jax.ShapeDtypeStruct((S5_GROUPS, S5_STATE), F32)
    gcp = jax.ShapeDtypeStruct((S5_GROUPS, S5_GROUP, S5_STATE), F32)
    return pl.pallas_call(
        _s5prep_body,
        out_shape=[gp, gp, gcp, gcp],
        name="s5prep",
    )(a_re, a_im, log_dt, b_re_t, b_im_t)


S5_LC = 64
S5_SLABS = D_MODEL // LANES
S5_HALF = LANES // S5_GROUP * S5_STATE


def _gelu_tanh(y):
    return 0.5 * y * (1.0 + jnp.tanh(math.sqrt(2.0 / math.pi) * (y + 0.044715 * (y * y * y))))


def _s5_body(x_ref, g_ref, bb_ref, cc_ref, lam_ref, d_ref, wglu_ref, o_ref,
             utb_ref, bu_ref, st_ref, z_ref, gtb_ref, *, batch):
    rows = S5_LC * batch

    @pl.when(pl.program_id(0) == 0)
    def _():
        st_ref[...] = jnp.zeros_like(st_ref)

    for b in range(batch):
        hb = _rms(x_ref[b], g_ref[...])
        for j in range(S5_SLABS):
            utb_ref[j, pl.ds(b, S5_LC, stride=batch), :] = hb[:, j * LANES:(j + 1) * LANES]

    for j in range(S5_SLABS):
        bu_ref[j] = jnp.dot(utb_ref[j].astype(BF16), bb_ref[j], preferred_element_type=F32)

    for j in range(S5_SLABS):
        ar = lam_ref[j, :, :S5_HALF]
        ai = lam_ref[j, :, S5_HALF:]

        def step(t, h, j=j, ar=ar, ai=ai):
            hr, hi = h
            rs = pl.ds(t * batch, batch)
            nr = ar * hr - ai * hi + bu_ref[j, rs, :S5_HALF]
            ni = ar * hi + ai * hr + bu_ref[j, rs, S5_HALF:]
            bu_ref[j, rs, :S5_HALF] = nr
            bu_ref[j, rs, S5_HALF:] = ni
            return nr, ni

        h = (st_ref[j, :, :S5_HALF], st_ref[j, :, S5_HALF:])
        for t in range(S5_LC):
            h = step(t, h)
        hr, hi = h
        st_ref[j, :, :S5_HALF] = hr
        st_ref[j, :, S5_HALF:] = hi

    for j in range(S5_SLABS):
        sl = slice(j * LANES, (j + 1) * LANES)
        y = jnp.dot(bu_ref[j].astype(BF16), cc_ref[j], preferred_element_type=F32)
        y = y + d_ref[:, sl] * utb_ref[j]
        z_ref[:, sl] = _gelu_tanh(y).astype(BF16)

    vg = jnp.dot(z_ref[...], wglu_ref[...], preferred_element_type=F32)
    glu = vg[:, :D_MODEL] * jax.nn.sigmoid(vg[:, D_MODEL:])
    for j in range(S5_SLABS):
        gtb_ref[j] = glu[:, j * LANES:(j + 1) * LANES]

    for b in range(batch):
        for j in range(S5_SLABS):
            sl = slice(j * LANES, (j + 1) * LANES)
            o_ref[b, :, sl] = x_ref[b, :, sl] + gtb_ref[j, pl.ds(b, S5_LC, stride=batch), :]


def _s5(x, g, bb, cc, lam, d_skip, w_glu):
    b, s, _ = x.shape
    assert b == SUBLANES
    rows = S5_LC * b
    blk = pl.BlockSpec((b, S5_LC, D_MODEL), lambda i: (0, i, 0))
    return pl.pallas_call(
        functools.partial(_s5_body, batch=b),
        grid=(s // S5_LC,),
        in_specs=[
            blk,
            _const_spec((1, D_MODEL)),
            _const_spec((S5_SLABS, LANES, 2 * S5_HALF)),
            _const_spec((S5_SLABS, 2 * S5_HALF, LANES)),
            _const_spec((S5_SLABS, SUBLANES, 2 * S5_HALF)),
            _const_spec((1, D_MODEL)),
            _const_spec((D_MODEL, 2 * D_MODEL)),
        ],
        out_specs=blk,
        out_shape=jax.ShapeDtypeStruct(x.shape, F32),
        scratch_shapes=[
            pltpu.VMEM((S5_SLABS, rows, LANES), F32),
            pltpu.VMEM((S5_SLABS, rows, 2 * S5_HALF), F32),
            pltpu.VMEM((S5_SLABS, SUBLANES, 2 * S5_HALF), F32),
            pltpu.VMEM((rows, D_MODEL), BF16),
            pltpu.VMEM((S5_SLABS, rows, LANES), F32),
        ],
        compiler_params=pltpu.CompilerParams(
            dimension_semantics=("arbitrary",), vmem_limit_bytes=VMEM_LIMIT),
        name="s5_glu",
    )(x, g, bb, cc, lam, d_skip, w_glu)


def _rotary_tables(positions):
    half = ROT_DIM // 2
    inv_freq = ROPE_THETA ** (-jnp.arange(0, ROT_DIM, 2, dtype=F32) / ROT_DIM)
    ang = inv_freq[:, None, None] * positions.astype(F32)[None]
    cs = jnp.concatenate([jnp.cos(ang), jnp.sin(ang)], axis=0).transpose(1, 2, 0)
    lane = jnp.arange(LANES) % HEAD_DIM
    freq = jnp.arange(half)[:, None]
    first = (lane[None, :] == freq).astype(F32)
    second = (lane[None, :] == freq + half).astype(F32)
    zeros = jnp.zeros_like(first)
    spread = jnp.concatenate([
        jnp.concatenate([first + second, zeros], axis=1),
        jnp.concatenate([zeros, second - first], axis=1),
    ], axis=0).astype(BF16)
    return cs, spread


def _s5_matrices(lbr, lbi, bbr, bbi, c_re, c_im):
    gl = LANES // S5_GROUP
    eye = jnp.eye(gl, dtype=F32)

    def in_blocks(m):
        m = m.reshape(S5_SLABS, gl, S5_GROUP, S5_STATE)
        return jnp.einsum('jgcp,gh->jgchp', m, eye).reshape(S5_SLABS, LANES, S5_HALF)

    def out_blocks(m):
        m = m.reshape(S5_SLABS, gl, S5_GROUP, S5_STATE)
        return jnp.einsum('jgcp,gh->jgphc', m, eye).reshape(S5_SLABS, S5_HALF, LANES)

    bb = jnp.concatenate([in_blocks(bbr), in_blocks(bbi)], axis=-1).astype(BF16)
    cc = jnp.concatenate([out_blocks(c_re), out_blocks(-c_im)], axis=1).astype(BF16)
    lam = jnp.concatenate([lbr.reshape(S5_SLABS, 1, S5_HALF), lbi.reshape(S5_SLABS, 1, S5_HALF)], axis=-1)
    lam = jnp.broadcast_to(lam, (S5_SLABS, SUBLANES, 2 * S5_HALF))
    return bb, cc, lam


def kernel(x, positions, ffn_norm, ffn_w_gate, ffn_w_up, ffn_w_down, mix_norm, ev_w_in, ev_q_norm, ev_k_norm, ev_pool_w, ev_pool_scale, ev_w_out, s5_a_re, s5_a_im, s5_log_dt, s5_b_re, s5_b_im, s5_c_re, s5_c_im, s5_d, s5_w_glu):
    b, s, d = x.shape
    t = b * s
    depth = ffn_norm.shape[0]
    cs, spread = _rotary_tables(positions)
    ones_bd = jnp.kron(jnp.eye(D_ATTN // HEAD_DIM, dtype=F32),
                       jnp.ones((HEAD_DIM, HEAD_DIM), F32)).astype(BF16)
    wg, wu, wd = ffn_w_gate.astype(BF16), ffn_w_up.astype(BF16), ffn_w_down.astype(BF16)
    w_in, pool_w, w_out = ev_w_in.astype(BF16), ev_pool_w.astype(BF16), ev_w_out.astype(BF16)

    def ffn(xx, layer, half):
        return _ffn(xx.reshape(t, d), ffn_norm[layer, half][None, :],
                    wg, wu, wd, layer, half).reshape(b, s, d)

    for layer in range(depth):
        x = ffn(x, layer, 0)
        j = layer // 2
        g = mix_norm[layer][None, :]
        if layer % 2 == 0:
            q, k, v, pool = _evproj(
                x, g, w_in, j,
                jnp.tile(ev_q_norm[j], D_ATTN // HEAD_DIM)[None, :],
                jnp.tile(ev_k_norm[j], D_ATTN // HEAD_DIM)[None, :],
                ones_bd, cs, spread, pool_w, ev_pool_scale[j][None, :])
            attn = _attn(q, k, v)
            x = _oproj(x, attn, pool, w_out, j)
        else:
            lbr, lbi, bbr, bbi = _s5prep(
                s5_a_re[j], s5_a_im[j], s5_log_dt[j][:, None],
                s5_b_re[j].transpose(0, 2, 1), s5_b_im[j].transpose(0, 2, 1))
            bb, cc, lam = _s5_matrices(lbr, lbi, bbr, bbi, s5_c_re[j], s5_c_im[j])
            x = _s5(x, g, bb, cc, lam, s5_d[j][None, :], s5_w_glu[j].astype(BF16))
        x = ffn(x, layer, 1)
    return x
```

```python
import functools
import math

import jax
import jax.numpy as jnp
from jax import lax
from jax.experimental import pallas as pl
from jax.experimental.pallas import tpu as pltpu

F32 = jnp.float32
BF16 = jnp.bfloat16

D_MODEL = 1024
D_ATTN = 512
HEAD_DIM = 64
ROT_DIM = 16
ROPE_THETA = 500000.0
DILATIONS = (1, 4, 16)
N_BACK = 128
D_POOL = 512
POOL_WINDOWS = (2, 4, 8, 16)
POOL_C = 128
S5_GROUP = 16
S5_GROUPS = 64
S5_STATE = 64
D_FF = 2816
EPS = 1e-6

LANES = 128
SUBLANES = 8
MXU_COLS = 256
VMEM_LIMIT = 56 * 1024 * 1024

NEG_BIG = -1e30


def _rms(x, g):
    return x * lax.rsqrt(jnp.mean(x * x, axis=-1, keepdims=True) + EPS) * g


def _const_spec(shape):
    nd = len(shape)
    return pl.BlockSpec(shape, lambda *_: (0,) * nd, pipeline_mode=pl.Buffered(1))


FFN_TM = 512


def _ffn_body(x_ref, g_ref, wg_ref, wu_ref, wd_ref, o_ref, a_ref):
    x = x_ref[...]
    h = _rms(x, g_ref[...]).astype(BF16)
    for c in range(D_FF // MXU_COLS):
        sl = slice(c * MXU_COLS, (c + 1) * MXU_COLS)
        gate = jnp.dot(h, wg_ref[:, sl], preferred_element_type=F32)
        up = jnp.dot(h, wu_ref[:, sl], preferred_element_type=F32)
        a_ref[:, sl] = (gate * jax.nn.sigmoid(gate) * up).astype(BF16)
    o_ref[...] = x + 0.5 * jnp.dot(a_ref[...], wd_ref[...], preferred_element_type=F32)


def _ffn(x2d, g, wg, wu, wd, layer, half):
    t = x2d.shape[0]
    pick = lambda i: (layer, half, 0, 0)
    single = pl.Buffered(1)
    return pl.pallas_call(
        _ffn_body,
        grid=(t // FFN_TM,),
        in_specs=[
            pl.BlockSpec((FFN_TM, D_MODEL), lambda i: (i, 0)),
            _const_spec((1, D_MODEL)),
            pl.BlockSpec((None, None, D_MODEL, D_FF), pick, pipeline_mode=single),
            pl.BlockSpec((None, None, D_MODEL, D_FF), pick, pipeline_mode=single),
            pl.BlockSpec((None, None, D_FF, D_MODEL), pick, pipeline_mode=single),
        ],
        out_specs=pl.BlockSpec((FFN_TM, D_MODEL), lambda i: (i, 0)),
        out_shape=jax.ShapeDtypeStruct((t, D_MODEL), F32),
        scratch_shapes=[pltpu.VMEM((FFN_TM, D_FF), BF16)],
        compiler_params=pltpu.CompilerParams(
            dimension_semantics=("arbitrary",), vmem_limit_bytes=VMEM_LIMIT),
        name="ffn",
    )(x2d, g, wg, wu, wd)


EV_TM = 512
POOL_HALO = 16
ATT_RES = 4


def _head_norm_rot(t, gn, ones_bd, c128, s128):
    ssq = jnp.dot((t * t).astype(BF16), ones_bd, preferred_element_type=F32)
    y = t * lax.rsqrt(ssq * (1.0 / HEAD_DIM) + EPS) * gn
    lane = lax.broadcasted_iota(jnp.int32, (1, LANES), 1) % HEAD_DIM
    outs = []
    for j in range(D_ATTN // LANES):
        yj = y[:, j * LANES:(j + 1) * LANES]
        swapped = jnp.where(lane < ROT_DIM // 2,
                            pltpu.roll(yj, LANES - ROT_DIM // 2, 1),
                            pltpu.roll(yj, ROT_DIM // 2, 1))
        outs.append(yj * c128 + swapped * s128)
    return outs


def _evproj_body(x_ref, g_ref, w_ref, qn_ref, kn_ref, ones_ref, cs_ref, e_ref,
                 pw_ref, ps_ref, q_ref, k_ref, v_ref, p_ref, carry_ref, stage_ref):
    it = pl.program_id(1)

    @pl.when(it == 0)
    def _():
        carry_ref[...] = jnp.zeros_like(carry_ref)

    x = x_ref[...]
    h = _rms(x, g_ref[...]).astype(BF16)
    proj = jnp.dot(h, w_ref[...], preferred_element_type=F32)
    ones_bd = ones_ref[...]
    cs = cs_ref[...]
    cs_hi = cs.astype(BF16)
    cs_lo = (cs - cs_hi.astype(F32)).astype(BF16)
    spread = (jnp.dot(cs_hi, e_ref[...], preferred_element_type=F32)
              + jnp.dot(cs_lo, e_ref[...], preferred_element_type=F32))
    lane = lax.broadcasted_iota(jnp.int32, (1, LANES), 1) % HEAD_DIM
    c128 = spread[:, :LANES] + jnp.where(lane < ROT_DIM, 0.0, 1.0)
    s128 = spread[:, LANES:]
    qs = _head_norm_rot(proj[:, :D_ATTN], qn_ref[...], ones_bd, c128, s128)
    ks = _head_norm_rot(proj[:, D_ATTN:2 * D_ATTN], kn_ref[...], ones_bd, c128, s128)
    sub = EV_TM // ATT_RES
    for a, (o_ref, slabs) in enumerate((
            (q_ref, qs), (k_ref, ks),
            (v_ref, [proj[:, 2 * D_ATTN + j * LANES:2 * D_ATTN + (j + 1) * LANES]
                     for j in range(D_ATTN // LANES)]))):
        for j in range(D_ATTN // LANES):
            stage_ref[a * (D_ATTN // LANES) + j] = slabs[j]
        for j in range(D_ATTN // LANES):
            for r in range(ATT_RES):
                o_ref[r, :, j * LANES:(j + 1) * LANES] = stage_ref[
                    a * (D_ATTN // LANES) + j, pl.ds(r, sub, stride=ATT_RES), :]

    p = proj[:, 3 * D_ATTN:]
    pe = jnp.concatenate([carry_ref[...], p], axis=0)
    carry_ref[...] = p[EV_TM - POOL_HALO:, :]
    pos1 = (it * EV_TM + 1 + lax.broadcasted_iota(jnp.int32, (EV_TM, 1), 0)).astype(F32)
    for g, w in enumerate(POOL_WINDOWS):
        sl = slice(g * POOL_C, (g + 1) * POOL_C)
        s = pe[:, sl]
        shift = 1
        while shift < w:
            s = s + pltpu.roll(s, shift, 0)
            shift *= 2
        count = jnp.minimum(pos1, float(w))
        pooled = s[POOL_HALO:, :] / count - p[:, sl]
        y = jnp.dot(pooled.astype(BF16), pw_ref[g], preferred_element_type=F32)
        p_ref[:, sl] = y * ps_ref[:, sl]


def _evproj(x, g, w_in, j, qn, kn, ones_bd, cs, spread, pool_w, pool_scale):
    b, s, _ = x.shape
    tok = lambda bi, i: (bi, i, 0)
    out = jax.ShapeDtypeStruct((b, s, D_ATTN), F32)
    res_out = jax.ShapeDtypeStruct((b, ATT_RES, s // ATT_RES, D_ATTN), F32)
    res_spec = pl.BlockSpec((None, ATT_RES, EV_TM // ATT_RES, D_ATTN), lambda bi, i: (bi, 0, i, 0))
    single = pl.Buffered(1)
    return pl.pallas_call(
        _evproj_body,
        grid=(b, s // EV_TM),
        in_specs=[
            pl.BlockSpec((None, EV_TM, D_MODEL), tok),
            _const_spec((1, D_MODEL)),
            pl.BlockSpec((None, D_MODEL, 4 * D_ATTN), lambda bi, i: (j, 0, 0), pipeline_mode=single),
            _const_spec((1, D_ATTN)),
            _const_spec((1, D_ATTN)),
            _const_spec((D_ATTN, D_ATTN)),
            pl.BlockSpec((None, EV_TM, ROT_DIM), tok),
            _const_spec((ROT_DIM, 2 * LANES)),
            pl.BlockSpec((None, len(POOL_WINDOWS), POOL_C, POOL_C), lambda bi, i: (j, 0, 0, 0),
                         pipeline_mode=single),
            _const_spec((1, D_POOL)),
        ],
        out_specs=[res_spec] * 3 + [pl.BlockSpec((None, EV_TM, D_ATTN), tok)],
        out_shape=[res_out] * 3 + [out],
        scratch_shapes=[pltpu.VMEM((POOL_HALO, D_POOL), F32),
                        pltpu.VMEM((3 * D_ATTN // LANES, EV_TM, LANES), F32)],
        compiler_params=pltpu.CompilerParams(
            dimension_semantics=("arbitrary", "arbitrary"), vmem_limit_bytes=VMEM_LIMIT),
        name="evproj",
    )(x, g, w_in, qn, kn, ones_bd, cs, spread, pool_w, pool_scale)


ATT_BLK = N_BACK


ATT_UNROLL = 8
ATT_QSCALE = HEAD_DIM ** -0.5 * math.log2(math.e)


def _attn_body(q_ref, k_ref, v_ref, o_ref, bias_ref, *stats, seq):
    lane = lax.broadcasted_iota(jnp.int32, (1, LANES), 1)
    head0 = lane < HEAD_DIM
    units = seq // ATT_BLK
    sub = seq // ATT_RES
    qc = ATT_BLK // ATT_RES
    kc = 2 * qc

    qi = lax.broadcasted_iota(jnp.int32, (ATT_BLK, 2 * ATT_BLK), 0)
    kj = lax.broadcasted_iota(jnp.int32, (ATT_BLK, 2 * ATT_BLK), 1)
    qpos = (qi % qc) * ATT_RES + qi // qc
    kpos = (kj % kc) * ATT_RES + kj // kc
    for idx, (dq, off) in enumerate(((qi - kj, 0), (qi - kj, ATT_BLK),
                                     (qpos - kpos, 0), (qpos - kpos, ATT_BLK))):
        delta = dq + off
        bias_ref[idx] = jnp.where((delta >= 0) & (delta <= N_BACK), 0.0, NEG_BIG)

    ones = jnp.ones((2 * ATT_BLK, LANES), BF16)

    def attend(q2, k2, v2, bias):
        q2 = q2 * ATT_QSCALE
        qq = jnp.concatenate([jnp.where(head0, q2, 0.0), jnp.where(head0, 0.0, q2)],
                             axis=0).astype(BF16)
        vaug = jnp.concatenate([v2.astype(BF16), ones], axis=1)
        sc = lax.dot_general(qq, k2.astype(BF16), (((1,), (1,)), ((), ())),
                             preferred_element_type=F32)
        sc = sc + jnp.concatenate([bias, bias], axis=0)
        m = jnp.max(sc, axis=-1, keepdims=True)
        p = jnp.exp2(sc - m).astype(BF16)
        res = jnp.dot(p, vaug, preferred_element_type=F32)
        return (jnp.where(head0, res[:ATT_BLK, :LANES], res[ATT_BLK:, :LANES]),
                jnp.where(head0, res[:ATT_BLK, LANES:], res[ATT_BLK:, LANES:]),
                jnp.where(head0, m[:ATT_BLK], m[ATT_BLK:]))

    def unit_consecutive(nb, refs):
        q0 = pl.multiple_of(nb * qc, qc)
        k0 = pl.multiple_of(jnp.maximum(nb - 1, 0) * qc, qc)
        q2 = jnp.concatenate([q_ref[c, pl.ds(q0, qc), :] for c in range(ATT_RES)], axis=0)
        k2 = jnp.concatenate([k_ref[c, pl.ds(k0, kc), :] for c in range(ATT_RES)], axis=0)
        v2 = jnp.concatenate([v_ref[c, pl.ds(k0, kc), :] for c in range(ATT_RES)], axis=0)
        outs = attend(q2, k2, v2, bias_ref[2 + jnp.minimum(nb, 1)])
        for ref, val in zip(refs, outs):
            for c in range(ATT_RES):
                ref[c, pl.ds(q0, qc), :] = val[c * qc:(c + 1) * qc]

    def unit_strided(u, refs, step):
        per_res = units // ATT_RES
        r = u // per_res
        w = u % per_res
        nblk = per_res // step
        rr = w // nblk
        nb = w % nblk
        kb = jnp.maximum(nb - 1, 0)

        def rows(blk, n):
            start = rr + step * ATT_BLK * blk
            return pl.ds(start, n) if step == 1 else pl.ds(start, n, stride=step)

        qrows = rows(nb, ATT_BLK)
        krows = rows(kb, 2 * ATT_BLK)
        outs = attend(q_ref[r, qrows, :], k_ref[r, krows, :], v_ref[r, krows, :],
                      bias_ref[jnp.minimum(nb, 1)])
        for ref, val in zip(refs, outs):
            ref[r, qrows, :] = val

    for pi, d in enumerate(DILATIONS):
        refs = stats[3 * pi:3 * pi + 3]

        def group(it, carry, d=d, refs=refs):
            for c in range(ATT_UNROLL):
                u = it * ATT_UNROLL + c
                if d == 1:
                    unit_consecutive(u, refs)
                else:
                    unit_strided(u, refs, d // ATT_RES)
            return carry

        lax.fori_loop(0, units // ATT_UNROLL, group, 0)

    def finish(i, carry):
        per_res = sub // ATT_BLK
        r = i // per_res
        rs = pl.ds(pl.multiple_of((i % per_res) * ATT_BLK, ATT_BLK), ATT_BLK)
        ms = [stats[3 * pi + 2][r, rs, :] for pi in range(len(DILATIONS))]
        m_tot = functools.reduce(jnp.maximum, ms)
        ws = [jnp.exp2(m - m_tot) for m in ms]
        num = sum(w * stats[3 * pi][r, rs, :] for pi, w in enumerate(ws))
        den = sum(w * stats[3 * pi + 1][r, rs, :] for pi, w in enumerate(ws))
        o_ref[r, rs, :] = num / den
        return carry

    lax.fori_loop(0, units, finish, 0)


def _attn(q, k, v):
    b, res, sub, _ = q.shape
    spec = pl.BlockSpec((None, res, sub, LANES), lambda bi, hp: (bi, 0, 0, hp))
    return pl.pallas_call(
        functools.partial(_attn_body, seq=res * sub),
        grid=(b, D_ATTN // LANES),
        in_specs=[spec] * 3,
        out_specs=spec,
        out_shape=jax.ShapeDtypeStruct(q.shape, F32),
        scratch_shapes=([pltpu.VMEM((4, ATT_BLK, 2 * ATT_BLK), F32)]
                        + [pltpu.VMEM((res, sub, LANES), F32)] * (3 * len(DILATIONS))),
        compiler_params=pltpu.CompilerParams(
            dimension_semantics=("arbitrary", "arbitrary"), vmem_limit_bytes=VMEM_LIMIT),
        name="dilated_attn",
    )(q, k, v)


OP_TM = 1024


def _oproj_body(x_ref, a_ref, p_ref, wa_ref, wp_ref, o_ref, nat_ref):
    sub = OP_TM // ATT_RES
    for j in range(D_ATTN // LANES):
        for r in range(ATT_RES):
            nat_ref[j, pl.ds(r, sub, stride=ATT_RES), :] = a_ref[r, :, j * LANES:(j + 1) * LANES]
    attn = jnp.concatenate([nat_ref[j] for j in range(D_ATTN // LANES)], axis=1)
    o_ref[...] = (x_ref[...]
                  + jnp.dot(attn.astype(BF16), wa_ref[...], preferred_element_type=F32)
                  + jnp.dot(p_ref[...].astype(BF16), wp_ref[...], preferred_element_type=F32))


def _oproj(x, attn, pool, w_out, j):
    b, s, _ = x.shape
    tok = lambda bi, i: (bi, i, 0)
    single = pl.Buffered(1)
    return pl.pallas_call(
        _oproj_body,
        grid=(b, s // OP_TM),
        in_specs=[
            pl.BlockSpec((None, OP_TM, D_MODEL), tok),
            pl.BlockSpec((None, ATT_RES, OP_TM // ATT_RES, D_ATTN), lambda bi, i: (bi, 0, i, 0)),
            pl.BlockSpec((None, OP_TM, D_POOL), tok),
            pl.BlockSpec((None, D_ATTN, D_MODEL), lambda bi, i: (j, 0, 0), pipeline_mode=single),
            pl.BlockSpec((None, D_POOL, D_MODEL), lambda bi, i: (j, 1, 0), pipeline_mode=single),
        ],
        out_specs=pl.BlockSpec((None, OP_TM, D_MODEL), tok),
        out_shape=jax.ShapeDtypeStruct(x.shape, F32),
        scratch_shapes=[pltpu.VMEM((D_ATTN // LANES, OP_TM, LANES), F32)],
        compiler_params=pltpu.CompilerParams(
            dimension_semantics=("arbitrary", "arbitrary"), vmem_limit_bytes=VMEM_LIMIT),
        name="oproj",
    )(x, attn, pool, w_out, w_out)


S5_CH = 4


def _cmul(ar, ai, br, bi):
    return ar * br - ai * bi, ar * bi + ai * br


def _s5prep_body(are_ref, aim_ref, ldt_ref, brt_ref, bit_ref, cre_ref, cim_ref,
                 lamk_re_ref, lamk_im_ref, win_re_ref, win_im_ref, wout_re_ref, wout_im_ref, kin_ref):
    lam_re = jnp.minimum(are_ref[...], -1e-4)
    lam_im = aim_ref[...]
    dt = jnp.exp(ldt_ref[...])
    mag = jnp.exp(lam_re * dt)
    lbr = mag * jnp.cos(lam_im * dt)
    lbi = mag * jnp.sin(lam_im * dt)
    den = lam_re * lam_re + lam_im * lam_im
    nre = lbr - 1.0
    cr = (nre * lam_re + lbi * lam_im) / den
    ci = (lbi * lam_re - nre * lam_im) / den
    bbr, bbi = _cmul(cr[:, None, :], ci[:, None, :], brt_ref[...], bit_ref[...])
    c_re = cre_ref[...]
    c_im = cim_ref[...]

    powers = [(jnp.ones_like(lbr), jnp.zeros_like(lbr))]
    for _ in range(S5_CH):
        powers.append(_cmul(powers[-1][0], powers[-1][1], lbr, lbi))
    lamk_re_ref[...], lamk_im_ref[...] = powers[S5_CH]

    contract_p = (((2,), (2,)), ((0,), (0,)))
    for i in range(S5_CH):
        pr, pi = powers[S5_CH - 1 - i]
        win_re_ref[i], win_im_ref[i] = _cmul(pr[:, None, :], pi[:, None, :], bbr, bbi)
        qr, qi = powers[i + 1]
        er, ei = _cmul(c_re, c_im, qr[:, None, :], qi[:, None, :])
        wout_re_ref[i] = er
        wout_im_ref[i] = -ei
        er, ei = _cmul(c_re, c_im, powers[i][0][:, None, :], powers[i][1][:, None, :])
        kin_ref[i] = (lax.dot_general(er, bbr, contract_p, precision=lax.Precision.HIGHEST,
                                      preferred_element_type=F32)
                      - lax.dot_general(ei, bbi, contract_p, precision=lax.Precision.HIGHEST,
                                        preferred_element_type=F32))


def _s5prep(a_re, a_im, log_dt, b_re_t, b_im_t, c_re, c_im):
    gp = jax.ShapeDtypeStruct((S5_GROUPS, S5_STATE), F32)
    kgcp = jax.ShapeDtypeStruct((S5_CH, S5_GROUPS, S5_GROUP, S5_STATE), F32)
    kgcc = jax.ShapeDtypeStruct((S5_CH, S5_GROUPS, S5_GROUP, S5_GROUP), F32)
    return pl.pallas_call(
        _s5prep_body,
        out_shape=[gp, gp, kgcp, kgcp, kgcp, kgcp, kgcc],
        name="s5prep",
    )(a_re, a_im, log_dt, b_re_t, b_im_t, c_re, c_im)


S5_LC = 64
S5_SLABS = D_MODEL // LANES
S5_HALF = LANES // S5_GROUP * S5_STATE


def _gelu_tanh(y):
    return 0.5 * y * (1.0 + jnp.tanh(math.sqrt(2.0 / math.pi) * (y + 0.044715 * (y * y * y))))


def _s5_body(x_ref, g_ref, win_ref, wout_ref, tin_ref, lamk_ref, d_ref, wglu_ref, o_ref,
             utb_ref, st_ref, carry_ref, z_ref, gtb_ref, *, batch):
    nch = S5_LC // S5_CH
    crows = nch * batch
    rows = S5_LC * batch

    @pl.when(pl.program_id(0) == 0)
    def _():
        carry_ref[...] = jnp.zeros_like(carry_ref)

    for b in range(batch):
        hb = _rms(x_ref[b], g_ref[...])
        for j in range(S5_SLABS):
            utb_ref[j, pl.ds(b, S5_LC, stride=batch), :] = hb[:, j * LANES:(j + 1) * LANES]

    for j in range(S5_SLABS):
        sl = slice(j * LANES, (j + 1) * LANES)
        u = utb_ref[j]
        u4 = u.reshape(nch, S5_CH, batch, LANES)
        ucat = jnp.concatenate([u4[:, i].reshape(crows, LANES) for i in range(S5_CH)],
                               axis=1).astype(BF16)
        st_ref[j] = jnp.dot(ucat, win_ref[j], preferred_element_type=F32)

        ar = lamk_ref[j, :, :S5_HALF]
        ai = lamk_ref[j, :, S5_HALF:]
        hr = carry_ref[j, :, :S5_HALF]
        hi = carry_ref[j, :, S5_HALF:]
        for c in range(nch):
            rs = pl.ds(c * batch, batch)
            gr = st_ref[j, rs, :S5_HALF]
            gi = st_ref[j, rs, S5_HALF:]
            st_ref[j, rs, :S5_HALF] = hr
            st_ref[j, rs, S5_HALF:] = hi
            hr, hi = ar * hr - ai * hi + gr, ar * hi + ai * hr + gi
        carry_ref[j, :, :S5_HALF] = hr
        carry_ref[j, :, S5_HALF:] = hi

        y4 = (jnp.dot(st_ref[j].astype(BF16), wout_ref[j], preferred_element_type=F32)
              + jnp.dot(ucat, tin_ref[j], preferred_element_type=F32))
        y = jnp.concatenate([y4[:, i * LANES:(i + 1) * LANES].reshape(nch, 1, batch, LANES)
                             for i in range(S5_CH)], axis=1).reshape(rows, LANES)
        y = y + d_ref[:, sl] * u
        z_ref[:, sl] = _gelu_tanh(y).astype(BF16)

    vg = jnp.dot(z_ref[...], wglu_ref[...], preferred_element_type=F32)
    glu = vg[:, :D_MODEL] * jax.nn.sigmoid(vg[:, D_MODEL:])
    for j in range(S5_SLABS):
        gtb_ref[j] = glu[:, j * LANES:(j + 1) * LANES]

    for b in range(batch):
        for j in range(S5_SLABS):
            sl = slice(j * LANES, (j + 1) * LANES)
            o_ref[b, :, sl] = x_ref[b, :, sl] + gtb_ref[j, pl.ds(b, S5_LC, stride=batch), :]


def _s5(x, g, win, wout, tin, lamk, d_skip, w_glu):
    b, s, _ = x.shape
    assert b == SUBLANES
    rows = S5_LC * b
    crows = rows // S5_CH
    blk = pl.BlockSpec((b, S5_LC, D_MODEL), lambda i: (0, i, 0))
    return pl.pallas_call(
        functools.partial(_s5_body, batch=b),
        grid=(s // S5_LC,),
        in_specs=[
            blk,
            _const_spec((1, D_MODEL)),
            _const_spec((S5_SLABS, S5_CH * LANES, 2 * S5_HALF)),
            _const_spec((S5_SLABS, 2 * S5_HALF, S5_CH * LANES)),
            _const_spec((S5_SLABS, S5_CH * LANES, S5_CH * LANES)),
            _const_spec((S5_SLABS, SUBLANES, 2 * S5_HALF)),
            _const_spec((1, D_MODEL)),
            _const_spec((D_MODEL, 2 * D_MODEL)),
        ],
        out_specs=blk,
        out_shape=jax.ShapeDtypeStruct(x.shape, F32),
        scratch_shapes=[
            pltpu.VMEM((S5_SLABS, rows, LANES), F32),
            pltpu.VMEM((S5_SLABS, crows, 2 * S5_HALF), F32),
            pltpu.VMEM((S5_SLABS, SUBLANES, 2 * S5_HALF), F32),
            pltpu.VMEM((rows, D_MODEL), BF16),
            pltpu.VMEM((S5_SLABS, rows, LANES), F32),
        ],
        compiler_params=pltpu.CompilerParams(
            dimension_semantics=("arbitrary",), vmem_limit_bytes=VMEM_LIMIT),
        name="s5_glu",
    )(x, g, win, wout, tin, lamk, d_skip, w_glu)


def _rotary_tables(positions):
    half = ROT_DIM // 2
    inv_freq = ROPE_THETA ** (-jnp.arange(0, ROT_DIM, 2, dtype=F32) / ROT_DIM)
    ang = inv_freq[:, None, None] * positions.astype(F32)[None]
    cs = jnp.concatenate([jnp.cos(ang), jnp.sin(ang)], axis=0).transpose(1, 2, 0)
    lane = jnp.arange(LANES) % HEAD_DIM
    freq = jnp.arange(half)[:, None]
    first = (lane[None, :] == freq).astype(F32)
    second = (lane[None, :] == freq + half).astype(F32)
    zeros = jnp.zeros_like(first)
    spread = jnp.concatenate([
        jnp.concatenate([first + second, zeros], axis=1),
        jnp.concatenate([zeros, second - first], axis=1),
    ], axis=0).astype(BF16)
    return cs, spread


def _s5_matrices(lamk_re, lamk_im, win_re, win_im, wout_re, wout_im, kin):
    gl = LANES // S5_GROUP
    eye = jnp.eye(gl, dtype=F32)
    k = S5_CH

    def in_blocks(m):
        m = m.reshape(k, S5_SLABS, gl, S5_GROUP, S5_STATE)
        return jnp.einsum('ijgdp,gh->jigdhp', m, eye).reshape(S5_SLABS, k * LANES, S5_HALF)

    def out_blocks(m):
        m = m.reshape(k, S5_SLABS, gl, S5_GROUP, S5_STATE)
        return jnp.einsum('ijgcp,gh->jgpihc', m, eye).reshape(S5_SLABS, S5_HALF, k * LANES)

    win = jnp.concatenate([in_blocks(win_re), in_blocks(win_im)], axis=-1).astype(BF16)
    wout = jnp.concatenate([out_blocks(wout_re), out_blocks(wout_im)], axis=1).astype(BF16)
    zero = jnp.zeros_like(kin[0])
    full = jnp.stack([jnp.stack([kin[bb - a] if bb >= a else zero for bb in range(k)]) for a in range(k)])
    full = full.reshape(k, k, S5_SLABS, gl, S5_GROUP, S5_GROUP)
    tin = jnp.einsum('abjgcd,gh->jagdbhc', full, eye).reshape(S5_SLABS, k * LANES, k * LANES).astype(BF16)
    lamk = jnp.concatenate([lamk_re.reshape(S5_SLABS, 1, S5_HALF), lamk_im.reshape(S5_SLABS, 1, S5_HALF)],
                           axis=-1)
    lamk = jnp.broadcast_to(lamk, (S5_SLABS, SUBLANES, 2 * S5_HALF))
    return win, wout, tin, lamk


def kernel(x, positions, ffn_norm, ffn_w_gate, ffn_w_up, ffn_w_down, mix_norm, ev_w_in, ev_q_norm, ev_k_norm, ev_pool_w, ev_pool_scale, ev_w_out, s5_a_re, s5_a_im, s5_log_dt, s5_b_re, s5_b_im, s5_c_re, s5_c_im, s5_d, s5_w_glu):
    b, s, d = x.shape
    t = b * s
    depth = ffn_norm.shape[0]
    cs, spread = _rotary_tables(positions)
    ones_bd = jnp.kron(jnp.eye(D_ATTN // HEAD_DIM, dtype=F32),
                       jnp.ones((HEAD_DIM, HEAD_DIM), F32)).astype(BF16)
    wg, wu, wd = ffn_w_gate.astype(BF16), ffn_w_up.astype(BF16), ffn_w_down.astype(BF16)
    w_in, pool_w, w_out = ev_w_in.astype(BF16), ev_pool_w.astype(BF16), ev_w_out.astype(BF16)

    def ffn(xx, layer, half):
        return _ffn(xx.reshape(t, d), ffn_norm[layer, half][None, :],
                    wg, wu, wd, layer, half).reshape(b, s, d)

    for layer in range(depth):
        x = ffn(x, layer, 0)
        j = layer // 2
        g = mix_norm[layer][None, :]
        if layer % 2 == 0:
            q, k, v, pool = _evproj(
                x, g, w_in, j,
                jnp.tile(ev_q_norm[j], D_ATTN // HEAD_DIM)[None, :],
                jnp.tile(ev_k_norm[j], D_ATTN // HEAD_DIM)[None, :],
                ones_bd, cs, spread, pool_w, ev_pool_scale[j][None, :])
            attn = _attn(q, k, v)
            x = _oproj(x, attn, pool, w_out, j)
        else:
            prep = _s5prep(
                s5_a_re[j], s5_a_im[j], s5_log_dt[j][:, None],
                s5_b_re[j].transpose(0, 2, 1), s5_b_im[j].transpose(0, 2, 1), s5_c_re[j], s5_c_im[j])
            win, wout, tin, lamk = _s5_matrices(*prep)
            x = _s5(x, g, win, wout, tin, lamk, s5_d[j][None, :], s5_w_glu[j].astype(BF16))
        x = ffn(x, layer, 1)
    return x
```

```python
import functools
import math

import jax
import jax.numpy as jnp
from jax import lax
from jax.experimental import pallas as pl
from jax.experimental.pallas import tpu as pltpu

F32 = jnp.float32
BF16 = jnp.bfloat16

D_MODEL = 1024
D_ATTN = 512
HEAD_DIM = 64
ROT_DIM = 16
ROPE_THETA = 500000.0
DILATIONS = (1, 4, 16)
N_BACK = 128
D_POOL = 512
POOL_WINDOWS = (2, 4, 8, 16)
POOL_C = 128
S5_GROUP = 16
S5_GROUPS = 64
S5_STATE = 64
D_FF = 2816
EPS = 1e-6

LANES = 128
SUBLANES = 8
MXU_COLS = 256
VMEM_LIMIT = 56 * 1024 * 1024

NEG_BIG = -1e30


def _rms(x, g):
    return x * lax.rsqrt(jnp.mean(x * x, axis=-1, keepdims=True) + EPS) * g


def _const_spec(shape):
    nd = len(shape)
    return pl.BlockSpec(shape, lambda *_: (0,) * nd, pipeline_mode=pl.Buffered(1))


FFN_TM = 512


def _ffn_body(x_ref, g_ref, wg_ref, wu_ref, wd_ref, o_ref, a_ref):
    x = x_ref[...]
    h = _rms(x, g_ref[...]).astype(BF16)
    for c in range(D_FF // MXU_COLS):
        sl = slice(c * MXU_COLS, (c + 1) * MXU_COLS)
        gate = jnp.dot(h, wg_ref[:, sl], preferred_element_type=F32)
        up = jnp.dot(h, wu_ref[:, sl], preferred_element_type=F32)
        a_ref[:, sl] = (gate * jax.nn.sigmoid(gate) * up).astype(BF16)
    o_ref[...] = x + 0.5 * jnp.dot(a_ref[...], wd_ref[...], preferred_element_type=F32)


def _ffn(x2d, g, wg, wu, wd, layer, half):
    t = x2d.shape[0]
    pick = lambda i: (layer, half, 0, 0)
    single = pl.Buffered(1)
    return pl.pallas_call(
        _ffn_body,
        grid=(t // FFN_TM,),
        in_specs=[
            pl.BlockSpec((FFN_TM, D_MODEL), lambda i: (i, 0)),
            _const_spec((1, D_MODEL)),
            pl.BlockSpec((None, None, D_MODEL, D_FF), pick, pipeline_mode=single),
            pl.BlockSpec((None, None, D_MODEL, D_FF), pick, pipeline_mode=single),
            pl.BlockSpec((None, None, D_FF, D_MODEL), pick, pipeline_mode=single),
        ],
        out_specs=pl.BlockSpec((FFN_TM, D_MODEL), lambda i: (i, 0)),
        out_shape=jax.ShapeDtypeStruct((t, D_MODEL), F32),
        scratch_shapes=[pltpu.VMEM((FFN_TM, D_FF), BF16)],
        compiler_params=pltpu.CompilerParams(
            dimension_semantics=("arbitrary",), vmem_limit_bytes=VMEM_LIMIT),
        name="ffn",
    )(x2d, g, wg, wu, wd)


EV_TM = 512
POOL_HALO = 16
ATT_RES = 4


def _head_norm_rot(t, gn, ones_bd, c128, s128):
    ssq = jnp.dot((t * t).astype(BF16), ones_bd, preferred_element_type=F32)
    y = t * lax.rsqrt(ssq * (1.0 / HEAD_DIM) + EPS) * gn
    lane = lax.broadcasted_iota(jnp.int32, (1, LANES), 1) % HEAD_DIM
    outs = []
    for j in range(D_ATTN // LANES):
        yj = y[:, j * LANES:(j + 1) * LANES]
        swapped = jnp.where(lane < ROT_DIM // 2,
                            pltpu.roll(yj, LANES - ROT_DIM // 2, 1),
                            pltpu.roll(yj, ROT_DIM // 2, 1))
        outs.append(yj * c128 + swapped * s128)
    return outs


def _evproj_body(x_ref, g_ref, w_ref, qn_ref, kn_ref, ones_ref, cs_ref, e_ref,
                 pw_ref, ps_ref, q_ref, k_ref, v_ref, p_ref, carry_ref, stage_ref):
    it = pl.program_id(1)

    @pl.when(it == 0)
    def _():
        carry_ref[...] = jnp.zeros_like(carry_ref)

    x = x_ref[...]
    h = _rms(x, g_ref[...]).astype(BF16)
    proj = jnp.dot(h, w_ref[...], preferred_element_type=F32)
    ones_bd = ones_ref[...]
    cs = cs_ref[...]
    cs_hi = cs.astype(BF16)
    cs_lo = (cs - cs_hi.astype(F32)).astype(BF16)
    spread = (jnp.dot(cs_hi, e_ref[...], preferred_element_type=F32)
              + jnp.dot(cs_lo, e_ref[...], preferred_element_type=F32))
    lane = lax.broadcasted_iota(jnp.int32, (1, LANES), 1) % HEAD_DIM
    c128 = spread[:, :LANES] + jnp.where(lane < ROT_DIM, 0.0, 1.0)
    s128 = spread[:, LANES:]
    qs = _head_norm_rot(proj[:, :D_ATTN], qn_ref[...], ones_bd, c128, s128)
    ks = _head_norm_rot(proj[:, D_ATTN:2 * D_ATTN], kn_ref[...], ones_bd, c128, s128)
    sub = EV_TM // ATT_RES
    for a, (o_ref, slabs) in enumerate((
            (q_ref, qs), (k_ref, ks),
            (v_ref, [proj[:, 2 * D_ATTN + j * LANES:2 * D_ATTN + (j + 1) * LANES]
                     for j in range(D_ATTN // LANES)]))):
        for j in range(D_ATTN // LANES):
            stage_ref[a * (D_ATTN // LANES) + j] = slabs[j]
        for j in range(D_ATTN // LANES):
            for r in range(ATT_RES):
                o_ref[r, :, j * LANES:(j + 1) * LANES] = stage_ref[
                    a * (D_ATTN // LANES) + j, pl.ds(r, sub, stride=ATT_RES), :]

    p = proj[:, 3 * D_ATTN:]
    pe = jnp.concatenate([carry_ref[...], p], axis=0)
    carry_ref[...] = p[EV_TM - POOL_HALO:, :]
    pos1 = (it * EV_TM + 1 + lax.broadcasted_iota(jnp.int32, (EV_TM, 1), 0)).astype(F32)
    for g, w in enumerate(POOL_WINDOWS):
        sl = slice(g * POOL_C, (g + 1) * POOL_C)
        s = pe[:, sl]
        shift = 1
        while shift < w:
            s = s + pltpu.roll(s, shift, 0)
            shift *= 2
        count = jnp.minimum(pos1, float(w))
        pooled = s[POOL_HALO:, :] / count - p[:, sl]
        y = jnp.dot(pooled.astype(BF16), pw_ref[g], preferred_element_type=F32)
        p_ref[:, sl] = y * ps_ref[:, sl]


def _evproj(x, g, w_in, j, qn, kn, ones_bd, cs, spread, pool_w, pool_scale):
    b, s, _ = x.shape
    tok = lambda bi, i: (bi, i, 0)
    out = jax.ShapeDtypeStruct((b, s, D_ATTN), F32)
    res_out = jax.ShapeDtypeStruct((b, ATT_RES, s // ATT_RES, D_ATTN), F32)
    res_spec = pl.BlockSpec((None, ATT_RES, EV_TM // ATT_RES, D_ATTN), lambda bi, i: (bi, 0, i, 0))
    single = pl.Buffered(1)
    return pl.pallas_call(
        _evproj_body,
        grid=(b, s // EV_TM),
        in_specs=[
            pl.BlockSpec((None, EV_TM, D_MODEL), tok),
            _const_spec((1, D_MODEL)),
            pl.BlockSpec((None, D_MODEL, 4 * D_ATTN), lambda bi, i: (j, 0, 0), pipeline_mode=single),
            _const_spec((1, D_ATTN)),
            _const_spec((1, D_ATTN)),
            _const_spec((D_ATTN, D_ATTN)),
            pl.BlockSpec((None, EV_TM, ROT_DIM), tok),
            _const_spec((ROT_DIM, 2 * LANES)),
            pl.BlockSpec((None, len(POOL_WINDOWS), POOL_C, POOL_C), lambda bi, i: (j, 0, 0, 0),
                         pipeline_mode=single),
            _const_spec((1, D_POOL)),
        ],
        out_specs=[res_spec] * 3 + [pl.BlockSpec((None, EV_TM, D_ATTN), tok)],
        out_shape=[res_out] * 3 + [out],
        scratch_shapes=[pltpu.VMEM((POOL_HALO, D_POOL), F32),
                        pltpu.VMEM((3 * D_ATTN // LANES, EV_TM, LANES), F32)],
        compiler_params=pltpu.CompilerParams(
            dimension_semantics=("arbitrary", "arbitrary"), vmem_limit_bytes=VMEM_LIMIT),
        name="evproj",
    )(x, g, w_in, qn, kn, ones_bd, cs, spread, pool_w, pool_scale)


ATT_BLK = N_BACK


ATT_UNROLL = 8
ATT_QSCALE = HEAD_DIM ** -0.5 * math.log2(math.e)


def _attn_body(q_ref, k_ref, v_ref, o_ref, bias_ref, *stats, seq):
    lane = lax.broadcasted_iota(jnp.int32, (1, LANES), 1)
    head0 = lane < HEAD_DIM
    units = seq // ATT_BLK
    sub = seq // ATT_RES
    qc = ATT_BLK // ATT_RES
    kc = 2 * qc

    qi = lax.broadcasted_iota(jnp.int32, (ATT_BLK, 2 * ATT_BLK), 0)
    kj = lax.broadcasted_iota(jnp.int32, (ATT_BLK, 2 * ATT_BLK), 1)
    qpos = (qi % qc) * ATT_RES + qi // qc
    kpos = (kj % kc) * ATT_RES + kj // kc
    for idx, (dq, off) in enumerate(((qi - kj, 0), (qi - kj, ATT_BLK),
                                     (qpos - kpos, 0), (qpos - kpos, ATT_BLK))):
        delta = dq + off
        bias_ref[idx] = jnp.where((delta >= 0) & (delta <= N_BACK), 0.0, NEG_BIG)

    ones = jnp.ones((2 * ATT_BLK, LANES), BF16)

    def attend(q2, k2, v2, bias):
        q2 = q2 * ATT_QSCALE
        qq = jnp.concatenate([jnp.where(head0, q2, 0.0), jnp.where(head0, 0.0, q2)],
                             axis=0).astype(BF16)
        vaug = jnp.concatenate([v2.astype(BF16), ones], axis=1)
        sc = lax.dot_general(qq, k2.astype(BF16), (((1,), (1,)), ((), ())),
                             preferred_element_type=F32)
        sc = sc + jnp.concatenate([bias, bias], axis=0)
        m = jnp.max(sc, axis=-1, keepdims=True)
        p = jnp.exp2(sc - m).astype(BF16)
        res = jnp.dot(p, vaug, preferred_element_type=F32)
        return (jnp.where(head0, res[:ATT_BLK, :LANES], res[ATT_BLK:, :LANES]),
                jnp.where(head0, res[:ATT_BLK, LANES:], res[ATT_BLK:, LANES:]),
                jnp.where(head0, m[:ATT_BLK], m[ATT_BLK:]))

    def unit_consecutive(nb, refs):
        q0 = pl.multiple_of(nb * qc, qc)
        k0 = pl.multiple_of(jnp.maximum(nb - 1, 0) * qc, qc)
        q2 = jnp.concatenate([q_ref[c, pl.ds(q0, qc), :] for c in range(ATT_RES)], axis=0)
        k2 = jnp.concatenate([k_ref[c, pl.ds(k0, kc), :] for c in range(ATT_RES)], axis=0)
        v2 = jnp.concatenate([v_ref[c, pl.ds(k0, kc), :] for c in range(ATT_RES)], axis=0)
        outs = attend(q2, k2, v2, bias_ref[2 + jnp.minimum(nb, 1)])
        for ref, val in zip(refs, outs):
            for c in range(ATT_RES):
                ref[c, pl.ds(q0, qc), :] = val[c * qc:(c + 1) * qc]

    def unit_strided(u, refs, step):
        per_res = units // ATT_RES
        r = u // per_res
        w = u % per_res
        nblk = per_res // step
        rr = w // nblk
        nb = w % nblk
        kb = jnp.maximum(nb - 1, 0)

        def rows(blk, n):
            start = rr + step * ATT_BLK * blk
            return pl.ds(start, n) if step == 1 else pl.ds(start, n, stride=step)

        qrows = rows(nb, ATT_BLK)
        krows = rows(kb, 2 * ATT_BLK)
        outs = attend(q_ref[r, qrows, :], k_ref[r, krows, :], v_ref[r, krows, :],
                      bias_ref[jnp.minimum(nb, 1)])
        for ref, val in zip(refs, outs):
            ref[r, qrows, :] = val

    for pi, d in enumerate(DILATIONS):
        refs = stats[3 * pi:3 * pi + 3]

        def group(it, carry, d=d, refs=refs):
            for c in range(ATT_UNROLL):
                u = it * ATT_UNROLL + c
                if d == 1:
                    unit_consecutive(u, refs)
                else:
                    unit_strided(u, refs, d // ATT_RES)
            return carry

        lax.fori_loop(0, units // ATT_UNROLL, group, 0)

    def finish(i, carry):
        per_res = sub // ATT_BLK
        r = i // per_res
        rs = pl.ds(pl.multiple_of((i % per_res) * ATT_BLK, ATT_BLK), ATT_BLK)
        ms = [stats[3 * pi + 2][r, rs, :] for pi in range(len(DILATIONS))]
        m_tot = functools.reduce(jnp.maximum, ms)
        ws = [jnp.exp2(m - m_tot) for m in ms]
        num = sum(w * stats[3 * pi][r, rs, :] for pi, w in enumerate(ws))
        den = sum(w * stats[3 * pi + 1][r, rs, :] for pi, w in enumerate(ws))
        o_ref[r, rs, :] = num / den
        return carry

    lax.fori_loop(0, units, finish, 0)


def _attn(q, k, v):
    b, res, sub, _ = q.shape
    spec = pl.BlockSpec((None, res, sub, LANES), lambda bi, hp: (bi, 0, 0, hp))
    return pl.pallas_call(
        functools.partial(_attn_body, seq=res * sub),
        grid=(b, D_ATTN // LANES),
        in_specs=[spec] * 3,
        out_specs=spec,
        out_shape=jax.ShapeDtypeStruct(q.shape, F32),
        scratch_shapes=([pltpu.VMEM((4, ATT_BLK, 2 * ATT_BLK), F32)]
                        + [pltpu.VMEM((res, sub, LANES), F32)] * (3 * len(DILATIONS))),
        compiler_params=pltpu.CompilerParams(
            dimension_semantics=("arbitrary", "arbitrary"), vmem_limit_bytes=VMEM_LIMIT),
        name="dilated_attn",
    )(q, k, v)


OP_TM = 1024


def _oproj_body(x_ref, a_ref, p_ref, wa_ref, wp_ref, o_ref, nat_ref):
    sub = OP_TM // ATT_RES
    for j in range(D_ATTN // LANES):
        for r in range(ATT_RES):
            nat_ref[j, pl.ds(r, sub, stride=ATT_RES), :] = a_ref[r, :, j * LANES:(j + 1) * LANES]
    attn = jnp.concatenate([nat_ref[j] for j in range(D_ATTN // LANES)], axis=1)
    o_ref[...] = (x_ref[...]
                  + jnp.dot(attn.astype(BF16), wa_ref[...], preferred_element_type=F32)
                  + jnp.dot(p_ref[...].astype(BF16), wp_ref[...], preferred_element_type=F32))


def _oproj(x, attn, pool, w_out, j):
    b, s, _ = x.shape
    tok = lambda bi, i: (bi, i, 0)
    single = pl.Buffered(1)
    return pl.pallas_call(
        _oproj_body,
        grid=(b, s // OP_TM),
        in_specs=[
            pl.BlockSpec((None, OP_TM, D_MODEL), tok),
            pl.BlockSpec((None, ATT_RES, OP_TM // ATT_RES, D_ATTN), lambda bi, i: (bi, 0, i, 0)),
            pl.BlockSpec((None, OP_TM, D_POOL), tok),
            pl.BlockSpec((None, D_ATTN, D_MODEL), lambda bi, i: (j, 0, 0), pipeline_mode=single),
            pl.BlockSpec((None, D_POOL, D_MODEL), lambda bi, i: (j, 1, 0), pipeline_mode=single),
        ],
        out_specs=pl.BlockSpec((None, OP_TM, D_MODEL), tok),
        out_shape=jax.ShapeDtypeStruct(x.shape, F32),
        scratch_shapes=[pltpu.VMEM((D_ATTN // LANES, OP_TM, LANES), F32)],
        compiler_params=pltpu.CompilerParams(
            dimension_semantics=("arbitrary", "arbitrary"), vmem_limit_bytes=VMEM_LIMIT),
        name="oproj",
    )(x, attn, pool, w_out, w_out)


S5_CH = 4


def _cmul(ar, ai, br, bi):
    return ar * br - ai * bi, ar * bi + ai * br


def _s5prep_body(are_ref, aim_ref, ldt_ref, brt_ref, bit_ref, cre_ref, cim_ref,
                 lamk_re_ref, lamk_im_ref, win_re_ref, win_im_ref, wout_re_ref, wout_im_ref, kin_ref):
    lam_re = jnp.minimum(are_ref[...], -1e-4)
    lam_im = aim_ref[...]
    dt = jnp.exp(ldt_ref[...])
    mag = jnp.exp(lam_re * dt)
    lbr = mag * jnp.cos(lam_im * dt)
    lbi = mag * jnp.sin(lam_im * dt)
    den = lam_re * lam_re + lam_im * lam_im
    nre = lbr - 1.0
    cr = (nre * lam_re + lbi * lam_im) / den
    ci = (lbi * lam_re - nre * lam_im) / den
    bbr, bbi = _cmul(cr[:, None, :], ci[:, None, :], brt_ref[...], bit_ref[...])
    c_re = cre_ref[...]
    c_im = cim_ref[...]

    powers = [(jnp.ones_like(lbr), jnp.zeros_like(lbr))]
    for _ in range(S5_CH):
        powers.append(_cmul(powers[-1][0], powers[-1][1], lbr, lbi))
    lamk_re_ref[...], lamk_im_ref[...] = powers[S5_CH]

    contract_p = (((2,), (2,)), ((0,), (0,)))
    for i in range(S5_CH):
        pr, pi = powers[S5_CH - 1 - i]
        win_re_ref[i], win_im_ref[i] = _cmul(pr[:, None, :], pi[:, None, :], bbr, bbi)
        qr, qi = powers[i + 1]
        er, ei = _cmul(c_re, c_im, qr[:, None, :], qi[:, None, :])
        wout_re_ref[i] = er
        wout_im_ref[i] = -ei
        er, ei = _cmul(c_re, c_im, powers[i][0][:, None, :], powers[i][1][:, None, :])
        kin_ref[i] = (lax.dot_general(bbr, er, contract_p, precision=lax.Precision.HIGHEST,
                                      preferred_element_type=F32)
                      - lax.dot_general(bbi, ei, contract_p, precision=lax.Precision.HIGHEST,
                                        preferred_element_type=F32))


def _s5prep(a_re, a_im, log_dt, b_re_t, b_im_t, c_re, c_im):
    gp = jax.ShapeDtypeStruct((S5_GROUPS, S5_STATE), F32)
    kgcp = jax.ShapeDtypeStruct((S5_CH, S5_GROUPS, S5_GROUP, S5_STATE), F32)
    kgcc = jax.ShapeDtypeStruct((S5_CH, S5_GROUPS, S5_GROUP, S5_GROUP), F32)
    return pl.pallas_call(
        _s5prep_body,
        out_shape=[gp, gp, kgcp, kgcp, kgcp, kgcp, kgcc],
        name="s5prep",
    )(a_re, a_im, log_dt, b_re_t, b_im_t, c_re, c_im)


S5_LC = 64
S5_SLABS = D_MODEL // LANES
S5_HALF = LANES // S5_GROUP * S5_STATE


def _gelu_tanh(y):
    return 0.5 * y * (1.0 + jnp.tanh(math.sqrt(2.0 / math.pi) * (y + 0.044715 * (y * y * y))))


def _s5_body(x_ref, g_ref, win_ref, wout_ref, tin_ref, lamk_ref, d_ref, wglu_ref, o_ref,
             utb_ref, st_ref, carry_ref, z_ref, gtb_ref, *, batch):
    nch = S5_LC // S5_CH
    crows = nch * batch
    rows = S5_LC * batch

    @pl.when(pl.program_id(0) == 0)
    def _():
        carry_ref[...] = jnp.zeros_like(carry_ref)

    for b in range(batch):
        hb = _rms(x_ref[b], g_ref[...])
        for j in range(S5_SLABS):
            utb_ref[j, pl.ds(b, S5_LC, stride=batch), :] = hb[:, j * LANES:(j + 1) * LANES]

    for j in range(S5_SLABS):
        sl = slice(j * LANES, (j + 1) * LANES)
        u = utb_ref[j]
        u4 = u.reshape(nch, S5_CH, batch, LANES)
        ucat = jnp.concatenate([u4[:, i].reshape(crows, LANES) for i in range(S5_CH)],
                               axis=1).astype(BF16)
        st_ref[j] = jnp.dot(ucat, win_ref[j], preferred_element_type=F32)

        ar = lamk_ref[j, :, :S5_HALF]
        ai = lamk_ref[j, :, S5_HALF:]
        hr = carry_ref[j, :, :S5_HALF]
        hi = carry_ref[j, :, S5_HALF:]
        for c in range(nch):
            rs = pl.ds(c * batch, batch)
            gr = st_ref[j, rs, :S5_HALF]
            gi = st_ref[j, rs, S5_HALF:]
            st_ref[j, rs, :S5_HALF] = hr
            st_ref[j, rs, S5_HALF:] = hi
            hr, hi = ar * hr - ai * hi + gr, ar * hi + ai * hr + gi
        carry_ref[j, :, :S5_HALF] = hr
        carry_ref[j, :, S5_HALF:] = hi

        y4 = (jnp.dot(st_ref[j].astype(BF16), wout_ref[j], preferred_element_type=F32)
              + jnp.dot(ucat, tin_ref[j], preferred_element_type=F32))
        y = jnp.concatenate([y4[:, i * LANES:(i + 1) * LANES].reshape(nch, 1, batch, LANES)
                             for i in range(S5_CH)], axis=1).reshape(rows, LANES)
        y = y + d_ref[:, sl] * u
        z_ref[:, sl] = _gelu_tanh(y).astype(BF16)

    z = z_ref[...]
    for c in range(D_MODEL // MXU_COLS):
        val = jnp.dot(z, wglu_ref[:, c * MXU_COLS:(c + 1) * MXU_COLS], preferred_element_type=F32)
        gate = jnp.dot(z, wglu_ref[:, D_MODEL + c * MXU_COLS:D_MODEL + (c + 1) * MXU_COLS],
                       preferred_element_type=F32)
        glu = val * jax.nn.sigmoid(gate)
        for h in range(MXU_COLS // LANES):
            gtb_ref[c * (MXU_COLS // LANES) + h] = glu[:, h * LANES:(h + 1) * LANES]

    for b in range(batch):
        for j in range(S5_SLABS):
            sl = slice(j * LANES, (j + 1) * LANES)
            o_ref[b, :, sl] = x_ref[b, :, sl] + gtb_ref[j, pl.ds(b, S5_LC, stride=batch), :]


def _s5(x, g, win, wout, tin, lamk, d_skip, w_glu):
    b, s, _ = x.shape
    assert b == SUBLANES
    rows = S5_LC * b
    crows = rows // S5_CH
    blk = pl.BlockSpec((b, S5_LC, D_MODEL), lambda i: (0, i, 0))
    return pl.pallas_call(
        functools.partial(_s5_body, batch=b),
        grid=(s // S5_LC,),
        in_specs=[
            blk,
            _const_spec((1, D_MODEL)),
            _const_spec((S5_SLABS, S5_CH * LANES, 2 * S5_HALF)),
            _const_spec((S5_SLABS, 2 * S5_HALF, S5_CH * LANES)),
            _const_spec((S5_SLABS, S5_CH * LANES, S5_CH * LANES)),
            _const_spec((S5_SLABS, SUBLANES, 2 * S5_HALF)),
            _const_spec((1, D_MODEL)),
            _const_spec((D_MODEL, 2 * D_MODEL)),
        ],
        out_specs=blk,
        out_shape=jax.ShapeDtypeStruct(x.shape, F32),
        scratch_shapes=[
            pltpu.VMEM((S5_SLABS, rows, LANES), F32),
            pltpu.VMEM((S5_SLABS, crows, 2 * S5_HALF), F32),
            pltpu.VMEM((S5_SLABS, SUBLANES, 2 * S5_HALF), F32),
            pltpu.VMEM((rows, D_MODEL), BF16),
            pltpu.VMEM((S5_SLABS, rows, LANES), F32),
        ],
        compiler_params=pltpu.CompilerParams(
            dimension_semantics=("arbitrary",), vmem_limit_bytes=VMEM_LIMIT),
        name="s5_glu",
    )(x, g, win, wout, tin, lamk, d_skip, w_glu)


def _rotary_tables(positions):
    half = ROT_DIM // 2
    inv_freq = ROPE_THETA ** (-jnp.arange(0, ROT_DIM, 2, dtype=F32) / ROT_DIM)
    ang = inv_freq[:, None, None] * positions.astype(F32)[None]
    cs = jnp.concatenate([jnp.cos(ang), jnp.sin(ang)], axis=0).transpose(1, 2, 0)
    lane = jnp.arange(LANES) % HEAD_DIM
    freq = jnp.arange(half)[:, None]
    first = (lane[None, :] == freq).astype(F32)
    second = (lane[None, :] == freq + half).astype(F32)
    zeros = jnp.zeros_like(first)
    spread = jnp.concatenate([
        jnp.concatenate([first + second, zeros], axis=1),
        jnp.concatenate([zeros, second - first], axis=1),
    ], axis=0).astype(BF16)
    return cs, spread


def _s5_matrices(lamk_re, lamk_im, win_re, win_im, wout_re, wout_im, kin_t):
    gl = LANES // S5_GROUP
    k = S5_CH

    def spread(m):
        w = m.shape[-1]
        rows = m.reshape(-1, w)
        rep = jnp.tile(jnp.eye(w, dtype=F32), (1, gl))
        wide = jnp.dot(rows, rep, precision=lax.Precision.HIGHEST)
        row_group = (jnp.arange(rows.shape[0]) // S5_GROUP) % gl
        col_group = jnp.arange(gl * w) // w
        wide = jnp.where(row_group[:, None] == col_group[None, :], wide, 0.0)
        return wide.reshape(k, S5_SLABS, LANES, gl * w)

    def chunk_rows(m):
        return m.transpose(1, 0, 2, 3).reshape(S5_SLABS, k * LANES, m.shape[-1])

    win = chunk_rows(jnp.concatenate([spread(win_re), spread(win_im)], axis=-1)).astype(BF16)
    wout = chunk_rows(jnp.concatenate([spread(wout_re), spread(wout_im)], axis=-1)).astype(BF16)
    wout = wout.swapaxes(-1, -2)
    e = spread(kin_t).astype(BF16)
    zero = jnp.zeros_like(e[0])
    tin = jnp.concatenate([jnp.concatenate([e[b - a] if b >= a else zero for b in range(k)], axis=-1)
                           for a in range(k)], axis=-2)
    lamk = jnp.concatenate([lamk_re.reshape(S5_SLABS, 1, S5_HALF), lamk_im.reshape(S5_SLABS, 1, S5_HALF)],
                           axis=-1)
    lamk = jnp.broadcast_to(lamk, (S5_SLABS, SUBLANES, 2 * S5_HALF))
    return win, wout, tin, lamk


def kernel(x, positions, ffn_norm, ffn_w_gate, ffn_w_up, ffn_w_down, mix_norm, ev_w_in, ev_q_norm, ev_k_norm, ev_pool_w, ev_pool_scale, ev_w_out, s5_a_re, s5_a_im, s5_log_dt, s5_b_re, s5_b_im, s5_c_re, s5_c_im, s5_d, s5_w_glu):
    b, s, d = x.shape
    t = b * s
    depth = ffn_norm.shape[0]
    cs, spread = _rotary_tables(positions)
    ones_bd = jnp.kron(jnp.eye(D_ATTN // HEAD_DIM, dtype=F32),
                       jnp.ones((HEAD_DIM, HEAD_DIM), F32)).astype(BF16)
    wg, wu, wd = ffn_w_gate.astype(BF16), ffn_w_up.astype(BF16), ffn_w_down.astype(BF16)
    w_in, pool_w, w_out = ev_w_in.astype(BF16), ev_pool_w.astype(BF16), ev_w_out.astype(BF16)

    def ffn(xx, layer, half):
        return _ffn(xx.reshape(t, d), ffn_norm[layer, half][None, :],
                    wg, wu, wd, layer, half).reshape(b, s, d)

    for layer in range(depth):
        x = ffn(x, layer, 0)
        j = layer // 2
        g = mix_norm[layer][None, :]
        if layer % 2 == 0:
            q, k, v, pool = _evproj(
                x, g, w_in, j,
                jnp.tile(ev_q_norm[j], D_ATTN // HEAD_DIM)[None, :],
                jnp.tile(ev_k_norm[j], D_ATTN // HEAD_DIM)[None, :],
                ones_bd, cs, spread, pool_w, ev_pool_scale[j][None, :])
            attn = _attn(q, k, v)
            x = _oproj(x, attn, pool, w_out, j)
        else:
            prep = _s5prep(
                s5_a_re[j], s5_a_im[j], s5_log_dt[j][:, None],
                s5_b_re[j].transpose(0, 2, 1), s5_b_im[j].transpose(0, 2, 1), s5_c_re[j], s5_c_im[j])
            win, wout, tin, lamk = _s5_matrices(*prep)
            x = _s5(x, g, win, wout, tin, lamk, s5_d[j][None, :], s5_w_glu[j].astype(BF16))
        x = ffn(x, layer, 1)
    return x
```

```python
import functools
import math

import jax
import jax.numpy as jnp
from jax import lax
from jax.experimental import pallas as pl
from jax.experimental.pallas import tpu as pltpu

F32 = jnp.float32
BF16 = jnp.bfloat16

D_MODEL = 1024
D_ATTN = 512
HEAD_DIM = 64
ROT_DIM = 16
ROPE_THETA = 500000.0
DILATIONS = (1, 4, 16)
N_BACK = 128
D_POOL = 512
POOL_WINDOWS = (2, 4, 8, 16)
POOL_C = 128
S5_GROUP = 16
S5_GROUPS = 64
S5_STATE = 64
D_FF = 2816
EPS = 1e-6

LANES = 128
SUBLANES = 8
MXU_COLS = 256
VMEM_LIMIT = 56 * 1024 * 1024

NEG_BIG = -1e30


def _rms(x, g):
    return x * lax.rsqrt(jnp.mean(x * x, axis=-1, keepdims=True) + EPS) * g


def _const_spec(shape):
    nd = len(shape)
    return pl.BlockSpec(shape, lambda *_: (0,) * nd, pipeline_mode=pl.Buffered(1))


FFN_TM = 512
ATT_RES = 4


def _ffn_body(*refs, mixer):
    if mixer:
        x_ref, at_ref, pool_ref, wa_ref, wp_ref, g_ref, wg_ref, wu_ref, wd_ref, o_ref, a_ref, nat_ref = refs
        sub = FFN_TM // ATT_RES
        for j in range(D_ATTN // LANES):
            for r in range(ATT_RES):
                nat_ref[j, pl.ds(r, sub, stride=ATT_RES), :] = at_ref[r, :, j * LANES:(j + 1) * LANES]
        attn = jnp.concatenate([nat_ref[j] for j in range(D_ATTN // LANES)], axis=1)
        x = (x_ref[...]
             + jnp.dot(attn.astype(BF16), wa_ref[...], preferred_element_type=F32)
             + jnp.dot(pool_ref[...].astype(BF16), wp_ref[...], preferred_element_type=F32))
    else:
        x_ref, g_ref, wg_ref, wu_ref, wd_ref, o_ref, a_ref = refs
        x = x_ref[...]
    h = _rms(x, g_ref[...]).astype(BF16)
    for c in range(D_FF // MXU_COLS):
        sl = slice(c * MXU_COLS, (c + 1) * MXU_COLS)
        gate = jnp.dot(h, wg_ref[:, sl], preferred_element_type=F32)
        up = jnp.dot(h, wu_ref[:, sl], preferred_element_type=F32)
        a_ref[:, sl] = (gate * jax.nn.sigmoid(gate) * up).astype(BF16)
    o_ref[...] = x + 0.5 * jnp.dot(a_ref[...], wd_ref[...], preferred_element_type=F32)


def _ffn(x2d, g, wg, wu, wd, layer, half, mixer=None):
    t = x2d.shape[0]
    pick = lambda i: (layer, half, 0, 0)
    single = pl.Buffered(1)
    row = lambda i: (i, 0)
    in_specs = [pl.BlockSpec((FFN_TM, D_MODEL), row)]
    args = [x2d]
    scratch = [pltpu.VMEM((FFN_TM, D_FF), BF16)]
    if mixer is not None:
        attn, pool, w_out, j = mixer
        per_seq = attn.shape[2] * ATT_RES // FFN_TM
        in_specs += [
            pl.BlockSpec((None, ATT_RES, FFN_TM // ATT_RES, D_ATTN),
                         lambda i: (i // per_seq, 0, i % per_seq, 0)),
            pl.BlockSpec((FFN_TM, D_POOL), row),
            pl.BlockSpec((None, D_ATTN, D_MODEL), lambda i: (j, 0, 0), pipeline_mode=single),
            pl.BlockSpec((None, D_POOL, D_MODEL), lambda i: (j, 1, 0), pipeline_mode=single),
        ]
        args += [attn, pool, w_out, w_out]
        scratch.append(pltpu.VMEM((D_ATTN // LANES, FFN_TM, LANES), F32))
    in_specs += [
        _const_spec((1, D_MODEL)),
        pl.BlockSpec((None, None, D_MODEL, D_FF), pick, pipeline_mode=single),
        pl.BlockSpec((None, None, D_MODEL, D_FF), pick, pipeline_mode=single),
        pl.BlockSpec((None, None, D_FF, D_MODEL), pick, pipeline_mode=single),
    ]
    args += [g, wg, wu, wd]
    return pl.pallas_call(
        functools.partial(_ffn_body, mixer=mixer is not None),
        grid=(t // FFN_TM,),
        in_specs=in_specs,
        out_specs=pl.BlockSpec((FFN_TM, D_MODEL), row),
        out_shape=jax.ShapeDtypeStruct((t, D_MODEL), F32),
        scratch_shapes=scratch,
        compiler_params=pltpu.CompilerParams(
            dimension_semantics=("arbitrary",), vmem_limit_bytes=VMEM_LIMIT),
        name="ffn_mix" if mixer is not None else "ffn",
    )(*args)


EV_TM = 512
POOL_HALO = 16


def _head_norm_rot(t, gn, ones_bd, c128, s128):
    ssq = jnp.dot((t * t).astype(BF16), ones_bd, preferred_element_type=F32)
    y = t * lax.rsqrt(ssq * (1.0 / HEAD_DIM) + EPS) * gn
    lane = lax.broadcasted_iota(jnp.int32, (1, LANES), 1) % HEAD_DIM
    outs = []
    for j in range(D_ATTN // LANES):
        yj = y[:, j * LANES:(j + 1) * LANES]
        swapped = jnp.where(lane < ROT_DIM // 2,
                            pltpu.roll(yj, LANES - ROT_DIM // 2, 1),
                            pltpu.roll(yj, ROT_DIM // 2, 1))
        outs.append(yj * c128 + swapped * s128)
    return outs


def _evproj_body(x_ref, g_ref, w_ref, qn_ref, kn_ref, ones_ref, cs_ref, e_ref,
                 pw_ref, ps_ref, q_ref, k_ref, v_ref, p_ref, carry_ref, stage_ref):
    it = pl.program_id(1)

    @pl.when(it == 0)
    def _():
        carry_ref[...] = jnp.zeros_like(carry_ref)

    x = x_ref[...]
    h = _rms(x, g_ref[...]).astype(BF16)
    proj = jnp.dot(h, w_ref[...], preferred_element_type=F32)
    ones_bd = ones_ref[...]
    cs = cs_ref[...]
    cs_hi = cs.astype(BF16)
    cs_lo = (cs - cs_hi.astype(F32)).astype(BF16)
    spread = (jnp.dot(cs_hi, e_ref[...], preferred_element_type=F32)
              + jnp.dot(cs_lo, e_ref[...], preferred_element_type=F32))
    lane = lax.broadcasted_iota(jnp.int32, (1, LANES), 1) % HEAD_DIM
    c128 = spread[:, :LANES] + jnp.where(lane < ROT_DIM, 0.0, 1.0)
    s128 = spread[:, LANES:]
    qs = _head_norm_rot(proj[:, :D_ATTN], qn_ref[...], ones_bd, c128, s128)
    ks = _head_norm_rot(proj[:, D_ATTN:2 * D_ATTN], kn_ref[...], ones_bd, c128, s128)
    sub = EV_TM // ATT_RES
    for a, (o_ref, slabs) in enumerate((
            (q_ref, qs), (k_ref, ks),
            (v_ref, [proj[:, 2 * D_ATTN + j * LANES:2 * D_ATTN + (j + 1) * LANES]
                     for j in range(D_ATTN // LANES)]))):
        for j in range(D_ATTN // LANES):
            stage_ref[a * (D_ATTN // LANES) + j] = slabs[j]
        for j in range(D_ATTN // LANES):
            for r in range(ATT_RES):
                o_ref[r, :, j * LANES:(j + 1) * LANES] = stage_ref[
                    a * (D_ATTN // LANES) + j, pl.ds(r, sub, stride=ATT_RES), :]

    p = proj[:, 3 * D_ATTN:]
    pe = jnp.concatenate([carry_ref[...], p], axis=0)
    carry_ref[...] = p[EV_TM - POOL_HALO:, :]
    pos1 = (it * EV_TM + 1 + lax.broadcasted_iota(jnp.int32, (EV_TM, 1), 0)).astype(F32)
    for g, w in enumerate(POOL_WINDOWS):
        sl = slice(g * POOL_C, (g + 1) * POOL_C)
        s = pe[:, sl]
        shift = 1
        while shift < w:
            s = s + pltpu.roll(s, shift, 0)
            shift *= 2
        count = jnp.minimum(pos1, float(w))
        pooled = s[POOL_HALO:, :] / count - p[:, sl]
        y = jnp.dot(pooled.astype(BF16), pw_ref[g], preferred_element_type=F32)
        p_ref[:, sl] = y * ps_ref[:, sl]


def _evproj(x, g, w_in, j, qn, kn, ones_bd, cs, spread, pool_w, pool_scale):
    b, s, _ = x.shape
    tok = lambda bi, i: (bi, i, 0)
    out = jax.ShapeDtypeStruct((b, s, D_ATTN), F32)
    res_out = jax.ShapeDtypeStruct((b, ATT_RES, s // ATT_RES, D_ATTN), F32)
    res_spec = pl.BlockSpec((None, ATT_RES, EV_TM // ATT_RES, D_ATTN), lambda bi, i: (bi, 0, i, 0))
    single = pl.Buffered(1)
    return pl.pallas_call(
        _evproj_body,
        grid=(b, s // EV_TM),
        in_specs=[
            pl.BlockSpec((None, EV_TM, D_MODEL), tok),
            _const_spec((1, D_MODEL)),
            pl.BlockSpec((None, D_MODEL, 4 * D_ATTN), lambda bi, i: (j, 0, 0), pipeline_mode=single),
            _const_spec((1, D_ATTN)),
            _const_spec((1, D_ATTN)),
            _const_spec((D_ATTN, D_ATTN)),
            pl.BlockSpec((None, EV_TM, ROT_DIM), tok),
            _const_spec((ROT_DIM, 2 * LANES)),
            pl.BlockSpec((None, len(POOL_WINDOWS), POOL_C, POOL_C), lambda bi, i: (j, 0, 0, 0),
                         pipeline_mode=single),
            _const_spec((1, D_POOL)),
        ],
        out_specs=[res_spec] * 3 + [pl.BlockSpec((None, EV_TM, D_ATTN), tok)],
        out_shape=[res_out] * 3 + [out],
        scratch_shapes=[pltpu.VMEM((POOL_HALO, D_POOL), F32),
                        pltpu.VMEM((3 * D_ATTN // LANES, EV_TM, LANES), F32)],
        compiler_params=pltpu.CompilerParams(
            dimension_semantics=("arbitrary", "arbitrary"), vmem_limit_bytes=VMEM_LIMIT),
        name="evproj",
    )(x, g, w_in, qn, kn, ones_bd, cs, spread, pool_w, pool_scale)


ATT_BLK = N_BACK


ATT_UNROLL = 8
ATT_QSCALE = HEAD_DIM ** -0.5 * math.log2(math.e)


def _attn_body(q_ref, k_ref, v_ref, o_ref, bias_ref, *stats, seq):
    lane = lax.broadcasted_iota(jnp.int32, (1, LANES), 1)
    head0 = lane < HEAD_DIM
    units = seq // ATT_BLK
    sub = seq // ATT_RES
    qc = ATT_BLK // ATT_RES
    kc = 2 * qc

    qi = lax.broadcasted_iota(jnp.int32, (ATT_BLK, 2 * ATT_BLK), 0)
    kj = lax.broadcasted_iota(jnp.int32, (ATT_BLK, 2 * ATT_BLK), 1)
    qpos = (qi % qc) * ATT_RES + qi // qc
    kpos = (kj % kc) * ATT_RES + kj // kc
    for idx, (dq, off) in enumerate(((qi - kj, 0), (qi - kj, ATT_BLK),
                                     (qpos - kpos, 0), (qpos - kpos, ATT_BLK))):
        delta = dq + off
        bias_ref[idx] = jnp.where((delta >= 0) & (delta <= N_BACK), 0.0, NEG_BIG)

    ones = jnp.ones((2 * ATT_BLK, LANES), BF16)

    def attend(q2, k2, v2, bias):
        q2 = q2 * ATT_QSCALE
        qq = jnp.concatenate([jnp.where(head0, q2, 0.0), jnp.where(head0, 0.0, q2)],
                             axis=0).astype(BF16)
        vaug = jnp.concatenate([v2.astype(BF16), ones], axis=1)
        sc = lax.dot_general(qq, k2.astype(BF16), (((1,), (1,)), ((), ())),
                             preferred_element_type=F32)
        sc = sc + jnp.concatenate([bias, bias], axis=0)
        m = jnp.max(sc, axis=-1, keepdims=True)
        p = jnp.exp2(sc - m).astype(BF16)
        res = jnp.dot(p, vaug, preferred_element_type=F32)
        return (jnp.where(head0, res[:ATT_BLK, :LANES], res[ATT_BLK:, :LANES]),
                jnp.where(head0, res[:ATT_BLK, LANES:], res[ATT_BLK:, LANES:]),
                jnp.where(head0, m[:ATT_BLK], m[ATT_BLK:]))

    def unit_consecutive(nb, refs):
        q0 = pl.multiple_of(nb * qc, qc)
        k0 = pl.multiple_of(jnp.maximum(nb - 1, 0) * qc, qc)
        q2 = jnp.concatenate([q_ref[c, pl.ds(q0, qc), :] for c in range(ATT_RES)], axis=0)
        k2 = jnp.concatenate([k_ref[c, pl.ds(k0, kc), :] for c in range(ATT_RES)], axis=0)
        v2 = jnp.concatenate([v_ref[c, pl.ds(k0, kc), :] for c in range(ATT_RES)], axis=0)
        outs = attend(q2, k2, v2, bias_ref[2 + jnp.minimum(nb, 1)])
        for ref, val in zip(refs, outs):
            for c in range(ATT_RES):
                ref[c, pl.ds(q0, qc), :] = val[c * qc:(c + 1) * qc]

    def unit_strided(u, refs, step):
        per_res = units // ATT_RES
        r = u // per_res
        w = u % per_res
        nblk = per_res // step
        rr = w // nblk
        nb = w % nblk
        kb = jnp.maximum(nb - 1, 0)

        def rows(blk, n):
            start = rr + step * ATT_BLK * blk
            return pl.ds(start, n) if step == 1 else pl.ds(start, n, stride=step)

        qrows = rows(nb, ATT_BLK)
        krows = rows(kb, 2 * ATT_BLK)
        outs = attend(q_ref[r, qrows, :], k_ref[r, krows, :], v_ref[r, krows, :],
                      bias_ref[jnp.minimum(nb, 1)])
        for ref, val in zip(refs, outs):
            ref[r, qrows, :] = val

    for pi, d in enumerate(DILATIONS):
        refs = stats[3 * pi:3 * pi + 3]

        def group(it, carry, d=d, refs=refs):
            for c in range(ATT_UNROLL):
                u = it * ATT_UNROLL + c
                if d == 1:
                    unit_consecutive(u, refs)
                else:
                    unit_strided(u, refs, d // ATT_RES)
            return carry

        lax.fori_loop(0, units // ATT_UNROLL, group, 0)

    def finish(i, carry):
        per_res = sub // ATT_BLK
        r = i // per_res
        rs = pl.ds(pl.multiple_of((i % per_res) * ATT_BLK, ATT_BLK), ATT_BLK)
        ms = [stats[3 * pi + 2][r, rs, :] for pi in range(len(DILATIONS))]
        m_tot = functools.reduce(jnp.maximum, ms)
        ws = [jnp.exp2(m - m_tot) for m in ms]
        num = sum(w * stats[3 * pi][r, rs, :] for pi, w in enumerate(ws))
        den = sum(w * stats[3 * pi + 1][r, rs, :] for pi, w in enumerate(ws))
        o_ref[r, rs, :] = num / den
        return carry

    lax.fori_loop(0, units, finish, 0)


def _attn(q, k, v):
    b, res, sub, _ = q.shape
    spec = pl.BlockSpec((None, res, sub, LANES), lambda bi, hp: (bi, 0, 0, hp))
    return pl.pallas_call(
        functools.partial(_attn_body, seq=res * sub),
        grid=(b, D_ATTN // LANES),
        in_specs=[spec] * 3,
        out_specs=spec,
        out_shape=jax.ShapeDtypeStruct(q.shape, F32),
        scratch_shapes=([pltpu.VMEM((4, ATT_BLK, 2 * ATT_BLK), F32)]
                        + [pltpu.VMEM((res, sub, LANES), F32)] * (3 * len(DILATIONS))),
        compiler_params=pltpu.CompilerParams(
            dimension_semantics=("arbitrary", "arbitrary"), vmem_limit_bytes=VMEM_LIMIT),
        name="dilated_attn",
    )(q, k, v)


S5_CH = 4


def _cmul(ar, ai, br, bi):
    return ar * br - ai * bi, ar * bi + ai * br


def _s5prep_body(are_ref, aim_ref, ldt_ref, brt_ref, bit_ref, cre_ref, cim_ref,
                 lamk_re_ref, lamk_im_ref, win_re_ref, win_im_ref, wout_re_ref, wout_im_ref, kin_ref):
    lam_re = jnp.minimum(are_ref[...], -1e-4)
    lam_im = aim_ref[...]
    dt = jnp.exp(ldt_ref[...])
    mag = jnp.exp(lam_re * dt)
    lbr = mag * jnp.cos(lam_im * dt)
    lbi = mag * jnp.sin(lam_im * dt)
    den = lam_re * lam_re + lam_im * lam_im
    nre = lbr - 1.0
    cr = (nre * lam_re + lbi * lam_im) / den
    ci = (lbi * lam_re - nre * lam_im) / den
    bbr, bbi = _cmul(cr[:, None, :], ci[:, None, :], brt_ref[...], bit_ref[...])
    c_re = cre_ref[...]
    c_im = cim_ref[...]

    powers = [(jnp.ones_like(lbr), jnp.zeros_like(lbr))]
    for _ in range(S5_CH):
        powers.append(_cmul(powers[-1][0], powers[-1][1], lbr, lbi))
    lamk_re_ref[...], lamk_im_ref[...] = powers[S5_CH]

    contract_p = (((2,), (2,)), ((0,), (0,)))
    for i in range(S5_CH):
        pr, pi = powers[S5_CH - 1 - i]
        win_re_ref[i], win_im_ref[i] = _cmul(pr[:, None, :], pi[:, None, :], bbr, bbi)
        qr, qi = powers[i + 1]
        er, ei = _cmul(c_re, c_im, qr[:, None, :], qi[:, None, :])
        wout_re_ref[i] = er
        wout_im_ref[i] = -ei
        er, ei = _cmul(c_re, c_im, powers[i][0][:, None, :], powers[i][1][:, None, :])
        kin_ref[i] = (lax.dot_general(bbr, er, contract_p, precision=lax.Precision.HIGHEST,
                                      preferred_element_type=F32)
                      - lax.dot_general(bbi, ei, contract_p, precision=lax.Precision.HIGHEST,
                                        preferred_element_type=F32))


def _s5prep(a_re, a_im, log_dt, b_re_t, b_im_t, c_re, c_im):
    gp = jax.ShapeDtypeStruct((S5_GROUPS, S5_STATE), F32)
    kgcp = jax.ShapeDtypeStruct((S5_CH, S5_GROUPS, S5_GROUP, S5_STATE), F32)
    kgcc = jax.ShapeDtypeStruct((S5_CH, S5_GROUPS, S5_GROUP, S5_GROUP), F32)
    return pl.pallas_call(
        _s5prep_body,
        out_shape=[gp, gp, kgcp, kgcp, kgcp, kgcp, kgcc],
        name="s5prep",
    )(a_re, a_im, log_dt, b_re_t, b_im_t, c_re, c_im)


S5_LC = 64
S5_SLABS = D_MODEL // LANES
S5_HALF = LANES // S5_GROUP * S5_STATE


def _gelu_tanh(y):
    return 0.5 * y * (1.0 + jnp.tanh(math.sqrt(2.0 / math.pi) * (y + 0.044715 * (y * y * y))))


def _s5_body(x_ref, g_ref, win_ref, wout_ref, tin_ref, lamk_ref, d_ref, wglu_ref, o_ref,
             utb_ref, st_ref, carry_ref, z_ref, gtb_ref, *, batch):
    nch = S5_LC // S5_CH
    crows = nch * batch
    rows = S5_LC * batch

    @pl.when(pl.program_id(0) == 0)
    def _():
        carry_ref[...] = jnp.zeros_like(carry_ref)

    for b in range(batch):
        hb = _rms(x_ref[b], g_ref[...])
        for j in range(S5_SLABS):
            utb_ref[j, pl.ds(b, S5_LC, stride=batch), :] = hb[:, j * LANES:(j + 1) * LANES]

    for j in range(S5_SLABS):
        sl = slice(j * LANES, (j + 1) * LANES)
        u = utb_ref[j]
        u4 = u.reshape(nch, S5_CH, batch, LANES)
        ucat = jnp.concatenate([u4[:, i].reshape(crows, LANES) for i in range(S5_CH)],
                               axis=1).astype(BF16)
        st_ref[j] = jnp.dot(ucat, win_ref[j], preferred_element_type=F32)

        ar = lamk_ref[j, :, :S5_HALF]
        ai = lamk_ref[j, :, S5_HALF:]
        hr = carry_ref[j, :, :S5_HALF]
        hi = carry_ref[j, :, S5_HALF:]
        for c in range(nch):
            rs = pl.ds(c * batch, batch)
            gr = st_ref[j, rs, :S5_HALF]
            gi = st_ref[j, rs, S5_HALF:]
            st_ref[j, rs, :S5_HALF] = hr
            st_ref[j, rs, S5_HALF:] = hi
            hr, hi = ar * hr - ai * hi + gr, ar * hi + ai * hr + gi
        carry_ref[j, :, :S5_HALF] = hr
        carry_ref[j, :, S5_HALF:] = hi

        y4 = (jnp.dot(st_ref[j].astype(BF16), wout_ref[j], preferred_element_type=F32)
              + jnp.dot(ucat, tin_ref[j], preferred_element_type=F32))
        y = jnp.concatenate([y4[:, i * LANES:(i + 1) * LANES].reshape(nch, 1, batch, LANES)
                             for i in range(S5_CH)], axis=1).reshape(rows, LANES)
        y = y + d_ref[:, sl] * u
        z_ref[:, sl] = _gelu_tanh(y).astype(BF16)

    z = z_ref[...]
    for c in range(D_MODEL // MXU_COLS):
        val = jnp.dot(z, wglu_ref[:, c * MXU_COLS:(c + 1) * MXU_COLS], preferred_element_type=F32)
        gate = jnp.dot(z, wglu_ref[:, D_MODEL + c * MXU_COLS:D_MODEL + (c + 1) * MXU_COLS],
                       preferred_element_type=F32)
        glu = val * jax.nn.sigmoid(gate)
        for h in range(MXU_COLS // LANES):
            gtb_ref[c * (MXU_COLS // LANES) + h] = glu[:, h * LANES:(h + 1) * LANES]

    for b in range(batch):
        for j in range(S5_SLABS):
            sl = slice(j * LANES, (j + 1) * LANES)
            o_ref[b, :, sl] = x_ref[b, :, sl] + gtb_ref[j, pl.ds(b, S5_LC, stride=batch), :]


def _s5(x, g, win, wout, tin, lamk, d_skip, w_glu):
    b, s, _ = x.shape
    assert b == SUBLANES
    rows = S5_LC * b
    crows = rows // S5_CH
    blk = pl.BlockSpec((b, S5_LC, D_MODEL), lambda i: (0, i, 0))
    return pl.pallas_call(
        functools.partial(_s5_body, batch=b),
        grid=(s // S5_LC,),
        in_specs=[
            blk,
            _const_spec((1, D_MODEL)),
            _const_spec((S5_SLABS, S5_CH * LANES, 2 * S5_HALF)),
            _const_spec((S5_SLABS, 2 * S5_HALF, S5_CH * LANES)),
            _const_spec((S5_SLABS, S5_CH * LANES, S5_CH * LANES)),
            _const_spec((S5_SLABS, SUBLANES, 2 * S5_HALF)),
            _const_spec((1, D_MODEL)),
            _const_spec((D_MODEL, 2 * D_MODEL)),
        ],
        out_specs=blk,
        out_shape=jax.ShapeDtypeStruct(x.shape, F32),
        scratch_shapes=[
            pltpu.VMEM((S5_SLABS, rows, LANES), F32),
            pltpu.VMEM((S5_SLABS, crows, 2 * S5_HALF), F32),
            pltpu.VMEM((S5_SLABS, SUBLANES, 2 * S5_HALF), F32),
            pltpu.VMEM((rows, D_MODEL), BF16),
            pltpu.VMEM((S5_SLABS, rows, LANES), F32),
        ],
        compiler_params=pltpu.CompilerParams(
            dimension_semantics=("arbitrary",), vmem_limit_bytes=VMEM_LIMIT),
        name="s5_glu",
    )(x, g, win, wout, tin, lamk, d_skip, w_glu)


def _rotary_tables(positions):
    half = ROT_DIM // 2
    inv_freq = ROPE_THETA ** (-jnp.arange(0, ROT_DIM, 2, dtype=F32) / ROT_DIM)
    ang = inv_freq[:, None, None] * positions.astype(F32)[None]
    cs = jnp.concatenate([jnp.cos(ang), jnp.sin(ang)], axis=0).transpose(1, 2, 0)
    lane = jnp.arange(LANES) % HEAD_DIM
    freq = jnp.arange(half)[:, None]
    first = (lane[None, :] == freq).astype(F32)
    second = (lane[None, :] == freq + half).astype(F32)
    zeros = jnp.zeros_like(first)
    spread = jnp.concatenate([
        jnp.concatenate([first + second, zeros], axis=1),
        jnp.concatenate([zeros, second - first], axis=1),
    ], axis=0).astype(BF16)
    return cs, spread


def _s5_matrices(lamk_re, lamk_im, win_re, win_im, wout_re, wout_im, kin_t):
    gl = LANES // S5_GROUP
    k = S5_CH

    def spread(m):
        w = m.shape[-1]
        rows = m.reshape(-1, w)
        rep = jnp.tile(jnp.eye(w, dtype=F32), (1, gl))
        wide = jnp.dot(rows, rep, precision=lax.Precision.HIGHEST)
        row_group = (jnp.arange(rows.shape[0]) // S5_GROUP) % gl
        col_group = jnp.arange(gl * w) // w
        wide = jnp.where(row_group[:, None] == col_group[None, :], wide, 0.0)
        return wide.reshape(k, S5_SLABS, LANES, gl * w)

    def chunk_rows(m):
        return m.transpose(1, 0, 2, 3).reshape(S5_SLABS, k * LANES, m.shape[-1])

    win = chunk_rows(jnp.concatenate([spread(win_re), spread(win_im)], axis=-1)).astype(BF16)
    wout = chunk_rows(jnp.concatenate([spread(wout_re), spread(wout_im)], axis=-1)).astype(BF16)
    wout = wout.swapaxes(-1, -2)
    e = spread(kin_t).astype(BF16)
    zero = jnp.zeros_like(e[0])
    tin = jnp.concatenate([jnp.concatenate([e[b - a] if b >= a else zero for b in range(k)], axis=-1)
                           for a in range(k)], axis=-2)
    lamk = jnp.concatenate([lamk_re.reshape(S5_SLABS, 1, S5_HALF), lamk_im.reshape(S5_SLABS, 1, S5_HALF)],
                           axis=-1)
    lamk = jnp.broadcast_to(lamk, (S5_SLABS, SUBLANES, 2 * S5_HALF))
    return win, wout, tin, lamk


def kernel(x, positions, ffn_norm, ffn_w_gate, ffn_w_up, ffn_w_down, mix_norm, ev_w_in, ev_q_norm, ev_k_norm, ev_pool_w, ev_pool_scale, ev_w_out, s5_a_re, s5_a_im, s5_log_dt, s5_b_re, s5_b_im, s5_c_re, s5_c_im, s5_d, s5_w_glu):
    b, s, d = x.shape
    t = b * s
    depth = ffn_norm.shape[0]
    cs, spread = _rotary_tables(positions)
    ones_bd = jnp.kron(jnp.eye(D_ATTN // HEAD_DIM, dtype=F32),
                       jnp.ones((HEAD_DIM, HEAD_DIM), F32)).astype(BF16)
    wg, wu, wd = ffn_w_gate.astype(BF16), ffn_w_up.astype(BF16), ffn_w_down.astype(BF16)
    w_in, pool_w, w_out = ev_w_in.astype(BF16), ev_pool_w.astype(BF16), ev_w_out.astype(BF16)

    def ffn(xx, layer, half, mixer=None):
        return _ffn(xx.reshape(t, d), ffn_norm[layer, half][None, :],
                    wg, wu, wd, layer, half, mixer).reshape(b, s, d)

    for layer in range(depth):
        x = ffn(x, layer, 0)
        j = layer // 2
        g = mix_norm[layer][None, :]
        mixer = None
        if layer % 2 == 0:
            q, k, v, pool = _evproj(
                x, g, w_in, j,
                jnp.tile(ev_q_norm[j], D_ATTN // HEAD_DIM)[None, :],
                jnp.tile(ev_k_norm[j], D_ATTN // HEAD_DIM)[None, :],
                ones_bd, cs, spread, pool_w, ev_pool_scale[j][None, :])
            mixer = (_attn(q, k, v), pool.reshape(t, D_POOL), w_out, j)
        else:
            prep = _s5prep(
                s5_a_re[j], s5_a_im[j], s5_log_dt[j][:, None],
                s5_b_re[j].transpose(0, 2, 1), s5_b_im[j].transpose(0, 2, 1), s5_c_re[j], s5_c_im[j])
            win, wout, tin, lamk = _s5_matrices(*prep)
            x = _s5(x, g, win, wout, tin, lamk, s5_d[j][None, :], s5_w_glu[j].astype(BF16))
        x = ffn(x, layer, 1, mixer)
    return x
```

```python
import functools
import math

import jax
import jax.numpy as jnp
from jax import lax
from jax.experimental import pallas as pl
from jax.experimental.pallas import tpu as pltpu

F32 = jnp.float32
BF16 = jnp.bfloat16

D_MODEL = 1024
D_ATTN = 512
HEAD_DIM = 64
ROT_DIM = 16
ROPE_THETA = 500000.0
DILATIONS = (1, 4, 16)
N_BACK = 128
D_POOL = 512
POOL_WINDOWS = (2, 4, 8, 16)
POOL_C = 128
S5_GROUP = 16
S5_GROUPS = 64
S5_STATE = 64
D_FF = 2816
EPS = 1e-6

LANES = 128
SUBLANES = 8
MXU_COLS = 256
VMEM_LIMIT = 56 * 1024 * 1024

NEG_BIG = -1e30


def _rms(x, g):
    return x * lax.rsqrt(jnp.mean(x * x, axis=-1, keepdims=True) + EPS) * g


def _const_spec(shape):
    nd = len(shape)
    return pl.BlockSpec(shape, lambda *_: (0,) * nd, pipeline_mode=pl.Buffered(1))


FFN_TM = 512
ATT_RES = 4


def _ffn_body(*refs, mixer):
    if mixer:
        x_ref, at_ref, pool_ref, wa_ref, wp_ref, g_ref, wg_ref, wu_ref, wd_ref, o_ref, a_ref, nat_ref = refs
        sub = FFN_TM // ATT_RES
        for j in range(D_ATTN // LANES):
            for r in range(ATT_RES):
                nat_ref[j, pl.ds(r, sub, stride=ATT_RES), :] = at_ref[r, :, j * LANES:(j + 1) * LANES]
        attn = jnp.concatenate([nat_ref[j] for j in range(D_ATTN // LANES)], axis=1)
        x = (x_ref[...]
             + jnp.dot(attn.astype(BF16), wa_ref[...], preferred_element_type=F32)
             + jnp.dot(pool_ref[...].astype(BF16), wp_ref[...], preferred_element_type=F32))
    else:
        x_ref, g_ref, wg_ref, wu_ref, wd_ref, o_ref, a_ref = refs
        x = x_ref[...]
    h = _rms(x, g_ref[...]).astype(BF16)
    for c in range(D_FF // MXU_COLS):
        sl = slice(c * MXU_COLS, (c + 1) * MXU_COLS)
        gate = jnp.dot(h, wg_ref[:, sl], preferred_element_type=F32)
        up = jnp.dot(h, wu_ref[:, sl], preferred_element_type=F32)
        a_ref[:, sl] = (gate * jax.nn.sigmoid(gate) * up).astype(BF16)
    o_ref[...] = x + 0.5 * jnp.dot(a_ref[...], wd_ref[...], preferred_element_type=F32)


def _ffn(x2d, g, wg, wu, wd, layer, half, mixer=None):
    t = x2d.shape[0]
    pick = lambda i: (layer, half, 0, 0)
    single = pl.Buffered(1)
    row = lambda i: (i, 0)
    in_specs = [pl.BlockSpec((FFN_TM, D_MODEL), row)]
    args = [x2d]
    scratch = [pltpu.VMEM((FFN_TM, D_FF), BF16)]
    if mixer is not None:
        attn, pool, w_out, j = mixer
        per_seq = attn.shape[2] * ATT_RES // FFN_TM
        in_specs += [
            pl.BlockSpec((None, ATT_RES, FFN_TM // ATT_RES, D_ATTN),
                         lambda i: (i // per_seq, 0, i % per_seq, 0)),
            pl.BlockSpec((FFN_TM, D_POOL), row),
            pl.BlockSpec((None, D_ATTN, D_MODEL), lambda i: (j, 0, 0), pipeline_mode=single),
            pl.BlockSpec((None, D_POOL, D_MODEL), lambda i: (j, 1, 0), pipeline_mode=single),
        ]
        args += [attn, pool, w_out, w_out]
        scratch.append(pltpu.VMEM((D_ATTN // LANES, FFN_TM, LANES), F32))
    in_specs += [
        _const_spec((1, D_MODEL)),
        pl.BlockSpec((None, None, D_MODEL, D_FF), pick, pipeline_mode=single),
        pl.BlockSpec((None, None, D_MODEL, D_FF), pick, pipeline_mode=single),
        pl.BlockSpec((None, None, D_FF, D_MODEL), pick, pipeline_mode=single),
    ]
    args += [g, wg, wu, wd]
    return pl.pallas_call(
        functools.partial(_ffn_body, mixer=mixer is not None),
        grid=(t // FFN_TM,),
        in_specs=in_specs,
        out_specs=pl.BlockSpec((FFN_TM, D_MODEL), row),
        out_shape=jax.ShapeDtypeStruct((t, D_MODEL), F32),
        scratch_shapes=scratch,
        compiler_params=pltpu.CompilerParams(
            dimension_semantics=("arbitrary",), vmem_limit_bytes=VMEM_LIMIT),
        name="ffn_mix" if mixer is not None else "ffn",
    )(*args)


EV_TM = 1024
POOL_HALO = 16


def _head_norm_rot(t, gn, ones_bd, c128, s128):
    ssq = jnp.dot((t * t).astype(BF16), ones_bd, preferred_element_type=F32)
    y = t * lax.rsqrt(ssq * (1.0 / HEAD_DIM) + EPS) * gn
    lane = lax.broadcasted_iota(jnp.int32, (1, LANES), 1) % HEAD_DIM
    outs = []
    for j in range(D_ATTN // LANES):
        yj = y[:, j * LANES:(j + 1) * LANES]
        swapped = jnp.where(lane < ROT_DIM // 2,
                            pltpu.roll(yj, LANES - ROT_DIM // 2, 1),
                            pltpu.roll(yj, ROT_DIM // 2, 1))
        outs.append(yj * c128 + swapped * s128)
    return outs


def _evproj_body(x_ref, g_ref, w_ref, qn_ref, kn_ref, ones_ref, cs_ref, e_ref,
                 pw_ref, ps_ref, q_ref, k_ref, v_ref, p_ref, carry_ref, stage_ref):
    it = pl.program_id(1)

    @pl.when(it == 0)
    def _():
        carry_ref[...] = jnp.zeros_like(carry_ref)

    x = x_ref[...]
    h = _rms(x, g_ref[...]).astype(BF16)
    proj = jnp.dot(h, w_ref[...], preferred_element_type=F32)
    ones_bd = ones_ref[...]
    cs = cs_ref[...]
    cs_hi = cs.astype(BF16)
    cs_lo = (cs - cs_hi.astype(F32)).astype(BF16)
    spread = (jnp.dot(cs_hi, e_ref[...], preferred_element_type=F32)
              + jnp.dot(cs_lo, e_ref[...], preferred_element_type=F32))
    lane = lax.broadcasted_iota(jnp.int32, (1, LANES), 1) % HEAD_DIM
    c128 = spread[:, :LANES] + jnp.where(lane < ROT_DIM, 0.0, 1.0)
    s128 = spread[:, LANES:]
    qs = _head_norm_rot(proj[:, :D_ATTN], qn_ref[...], ones_bd, c128, s128)
    ks = _head_norm_rot(proj[:, D_ATTN:2 * D_ATTN], kn_ref[...], ones_bd, c128, s128)
    sub = EV_TM // ATT_RES
    for a, (o_ref, slabs) in enumerate((
            (q_ref, qs), (k_ref, ks),
            (v_ref, [proj[:, 2 * D_ATTN + j * LANES:2 * D_ATTN + (j + 1) * LANES]
                     for j in range(D_ATTN // LANES)]))):
        for j in range(D_ATTN // LANES):
            stage_ref[a * (D_ATTN // LANES) + j] = slabs[j]
        for j in range(D_ATTN // LANES):
            for r in range(ATT_RES):
                o_ref[r, :, j * LANES:(j + 1) * LANES] = stage_ref[
                    a * (D_ATTN // LANES) + j, pl.ds(r, sub, stride=ATT_RES), :]

    p = proj[:, 3 * D_ATTN:]
    pe = jnp.concatenate([carry_ref[...], p], axis=0)
    carry_ref[...] = p[EV_TM - POOL_HALO:, :]
    pos1 = (it * EV_TM + 1 + lax.broadcasted_iota(jnp.int32, (EV_TM, 1), 0)).astype(F32)
    for g, w in enumerate(POOL_WINDOWS):
        sl = slice(g * POOL_C, (g + 1) * POOL_C)
        s = pe[:, sl]
        shift = 1
        while shift < w:
            s = s + pltpu.roll(s, shift, 0)
            shift *= 2
        count = jnp.minimum(pos1, float(w))
        pooled = s[POOL_HALO:, :] / count - p[:, sl]
        y = jnp.dot(pooled.astype(BF16), pw_ref[g], preferred_element_type=F32)
        p_ref[:, sl] = y * ps_ref[:, sl]


def _evproj(x, g, w_in, j, qn, kn, ones_bd, cs, spread, pool_w, pool_scale):
    b, s, _ = x.shape
    tok = lambda bi, i: (bi, i, 0)
    out = jax.ShapeDtypeStruct((b, s, D_ATTN), F32)
    res_out = jax.ShapeDtypeStruct((b, ATT_RES, s // ATT_RES, D_ATTN), F32)
    res_spec = pl.BlockSpec((None, ATT_RES, EV_TM // ATT_RES, D_ATTN), lambda bi, i: (bi, 0, i, 0))
    single = pl.Buffered(1)
    return pl.pallas_call(
        _evproj_body,
        grid=(b, s // EV_TM),
        in_specs=[
            pl.BlockSpec((None, EV_TM, D_MODEL), tok),
            _const_spec((1, D_MODEL)),
            pl.BlockSpec((None, D_MODEL, 4 * D_ATTN), lambda bi, i: (j, 0, 0), pipeline_mode=single),
            _const_spec((1, D_ATTN)),
            _const_spec((1, D_ATTN)),
            _const_spec((D_ATTN, D_ATTN)),
            pl.BlockSpec((None, EV_TM, ROT_DIM), tok),
            _const_spec((ROT_DIM, 2 * LANES)),
            pl.BlockSpec((None, len(POOL_WINDOWS), POOL_C, POOL_C), lambda bi, i: (j, 0, 0, 0),
                         pipeline_mode=single),
            _const_spec((1, D_POOL)),
        ],
        out_specs=[res_spec] * 3 + [pl.BlockSpec((None, EV_TM, D_ATTN), tok)],
        out_shape=[res_out] * 3 + [out],
        scratch_shapes=[pltpu.VMEM((POOL_HALO, D_POOL), F32),
                        pltpu.VMEM((3 * D_ATTN // LANES, EV_TM, LANES), F32)],
        compiler_params=pltpu.CompilerParams(
            dimension_semantics=("arbitrary", "arbitrary"), vmem_limit_bytes=VMEM_LIMIT),
        name="evproj",
    )(x, g, w_in, qn, kn, ones_bd, cs, spread, pool_w, pool_scale)


ATT_BLK = N_BACK


ATT_UNROLL = 8
ATT_QSCALE = HEAD_DIM ** -0.5 * math.log2(math.e)


def _attn_body(q_ref, k_ref, v_ref, o_ref, bias_ref, *stats, seq):
    lane = lax.broadcasted_iota(jnp.int32, (1, LANES), 1)
    head0 = lane < HEAD_DIM
    units = seq // ATT_BLK
    sub = seq // ATT_RES
    qc = ATT_BLK // ATT_RES
    kc = 2 * qc

    qi = lax.broadcasted_iota(jnp.int32, (ATT_BLK, 2 * ATT_BLK), 0)
    kj = lax.broadcasted_iota(jnp.int32, (ATT_BLK, 2 * ATT_BLK), 1)
    qpos = (qi % qc) * ATT_RES + qi // qc
    kpos = (kj % kc) * ATT_RES + kj // kc
    for idx, (dq, off) in enumerate(((qi - kj, 0), (qi - kj, ATT_BLK),
                                     (qpos - kpos, 0), (qpos - kpos, ATT_BLK))):
        delta = dq + off
        bias_ref[idx] = jnp.where((delta >= 0) & (delta <= N_BACK), 0.0, NEG_BIG)

    ones = jnp.ones((2 * ATT_BLK, LANES), BF16)

    def attend(q2, k2, v2, bias):
        q2 = q2 * ATT_QSCALE
        qq = jnp.concatenate([jnp.where(head0, q2, 0.0), jnp.where(head0, 0.0, q2)],
                             axis=0).astype(BF16)
        vaug = jnp.concatenate([v2.astype(BF16), ones], axis=1)
        sc = lax.dot_general(qq, k2.astype(BF16), (((1,), (1,)), ((), ())),
                             preferred_element_type=F32)
        sc = sc + jnp.concatenate([bias, bias], axis=0)
        m = jnp.max(sc, axis=-1, keepdims=True)
        p = jnp.exp2(sc - m).astype(BF16)
        res = jnp.dot(p, vaug, preferred_element_type=F32)
        return (jnp.where(head0, res[:ATT_BLK, :LANES], res[ATT_BLK:, :LANES]),
                jnp.where(head0, res[:ATT_BLK, LANES:], res[ATT_BLK:, LANES:]),
                jnp.where(head0, m[:ATT_BLK], m[ATT_BLK:]))

    def unit_consecutive(nb, refs):
        q0 = pl.multiple_of(nb * qc, qc)
        k0 = pl.multiple_of(jnp.maximum(nb - 1, 0) * qc, qc)
        q2 = jnp.concatenate([q_ref[c, pl.ds(q0, qc), :] for c in range(ATT_RES)], axis=0)
        k2 = jnp.concatenate([k_ref[c, pl.ds(k0, kc), :] for c in range(ATT_RES)], axis=0)
        v2 = jnp.concatenate([v_ref[c, pl.ds(k0, kc), :] for c in range(ATT_RES)], axis=0)
        outs = attend(q2, k2, v2, bias_ref[2 + jnp.minimum(nb, 1)])
        for ref, val in zip(refs, outs):
            for c in range(ATT_RES):
                ref[c, pl.ds(q0, qc), :] = val[c * qc:(c + 1) * qc]

    def unit_strided(u, refs, step):
        per_res = units // ATT_RES
        r = u // per_res
        w = u % per_res
        nblk = per_res // step
        rr = w // nblk
        nb = w % nblk
        kb = jnp.maximum(nb - 1, 0)

        def rows(blk, n):
            start = rr + step * ATT_BLK * blk
            return pl.ds(start, n) if step == 1 else pl.ds(start, n, stride=step)

        qrows = rows(nb, ATT_BLK)
        krows = rows(kb, 2 * ATT_BLK)
        outs = attend(q_ref[r, qrows, :], k_ref[r, krows, :], v_ref[r, krows, :],
                      bias_ref[jnp.minimum(nb, 1)])
        for ref, val in zip(refs, outs):
            ref[r, qrows, :] = val

    for pi, d in enumerate(DILATIONS):
        refs = stats[3 * pi:3 * pi + 3]

        def group(it, carry, d=d, refs=refs):
            for c in range(ATT_UNROLL):
                u = it * ATT_UNROLL + c
                if d == 1:
                    unit_consecutive(u, refs)
                else:
                    unit_strided(u, refs, d // ATT_RES)
            return carry

        lax.fori_loop(0, units // ATT_UNROLL, group, 0)

    def finish(i, carry):
        per_res = sub // ATT_BLK
        r = i // per_res
        rs = pl.ds(pl.multiple_of((i % per_res) * ATT_BLK, ATT_BLK), ATT_BLK)
        ms = [stats[3 * pi + 2][r, rs, :] for pi in range(len(DILATIONS))]
        m_tot = functools.reduce(jnp.maximum, ms)
        ws = [jnp.exp2(m - m_tot) for m in ms]
        num = sum(w * stats[3 * pi][r, rs, :] for pi, w in enumerate(ws))
        den = sum(w * stats[3 * pi + 1][r, rs, :] for pi, w in enumerate(ws))
        o_ref[r, rs, :] = num / den
        return carry

    lax.fori_loop(0, units, finish, 0)


def _attn(q, k, v):
    b, res, sub, _ = q.shape
    spec = pl.BlockSpec((None, res, sub, LANES), lambda bi, hp: (bi, 0, 0, hp))
    return pl.pallas_call(
        functools.partial(_attn_body, seq=res * sub),
        grid=(b, D_ATTN // LANES),
        in_specs=[spec] * 3,
        out_specs=spec,
        out_shape=jax.ShapeDtypeStruct(q.shape, F32),
        scratch_shapes=([pltpu.VMEM((4, ATT_BLK, 2 * ATT_BLK), F32)]
                        + [pltpu.VMEM((res, sub, LANES), F32)] * (3 * len(DILATIONS))),
        compiler_params=pltpu.CompilerParams(
            dimension_semantics=("arbitrary", "arbitrary"), vmem_limit_bytes=VMEM_LIMIT),
        name="dilated_attn",
    )(q, k, v)


S5_CH = 4


def _cmul(ar, ai, br, bi):
    return ar * br - ai * bi, ar * bi + ai * br


def _s5prep_body(are_ref, aim_ref, ldt_ref, brt_ref, bit_ref, cre_ref, cim_ref,
                 lamk_re_ref, lamk_im_ref, win_re_ref, win_im_ref, wout_re_ref, wout_im_ref, kin_ref):
    lam_re = jnp.minimum(are_ref[...], -1e-4)
    lam_im = aim_ref[...]
    dt = jnp.exp(ldt_ref[...])
    mag = jnp.exp(lam_re * dt)
    lbr = mag * jnp.cos(lam_im * dt)
    lbi = mag * jnp.sin(lam_im * dt)
    den = lam_re * lam_re + lam_im * lam_im
    nre = lbr - 1.0
    cr = (nre * lam_re + lbi * lam_im) / den
    ci = (lbi * lam_re - nre * lam_im) / den
    bbr, bbi = _cmul(cr[:, None, :], ci[:, None, :], brt_ref[...], bit_ref[...])
    c_re = cre_ref[...]
    c_im = cim_ref[...]

    powers = [(jnp.ones_like(lbr), jnp.zeros_like(lbr))]
    for _ in range(S5_CH):
        powers.append(_cmul(powers[-1][0], powers[-1][1], lbr, lbi))
    lamk_re_ref[...], lamk_im_ref[...] = powers[S5_CH]

    contract_p = (((2,), (2,)), ((0,), (0,)))
    for i in range(S5_CH):
        pr, pi = powers[S5_CH - 1 - i]
        win_re_ref[i], win_im_ref[i] = _cmul(pr[:, None, :], pi[:, None, :], bbr, bbi)
        qr, qi = powers[i + 1]
        er, ei = _cmul(c_re, c_im, qr[:, None, :], qi[:, None, :])
        wout_re_ref[i] = er
        wout_im_ref[i] = -ei
        er, ei = _cmul(c_re, c_im, powers[i][0][:, None, :], powers[i][1][:, None, :])
        kin_ref[i] = (lax.dot_general(bbr, er, contract_p, precision=lax.Precision.HIGHEST,
                                      preferred_element_type=F32)
                      - lax.dot_general(bbi, ei, contract_p, precision=lax.Precision.HIGHEST,
                                        preferred_element_type=F32))


def _s5prep(a_re, a_im, log_dt, b_re_t, b_im_t, c_re, c_im):
    gp = jax.ShapeDtypeStruct((S5_GROUPS, S5_STATE), F32)
    kgcp = jax.ShapeDtypeStruct((S5_CH, S5_GROUPS, S5_GROUP, S5_STATE), F32)
    kgcc = jax.ShapeDtypeStruct((S5_CH, S5_GROUPS, S5_GROUP, S5_GROUP), F32)
    return pl.pallas_call(
        _s5prep_body,
        out_shape=[gp, gp, kgcp, kgcp, kgcp, kgcp, kgcc],
        name="s5prep",
    )(a_re, a_im, log_dt, b_re_t, b_im_t, c_re, c_im)


S5_LC = 64
S5_SLABS = D_MODEL // LANES
S5_HALF = LANES // S5_GROUP * S5_STATE


def _gelu_tanh(y):
    return 0.5 * y * (1.0 + jnp.tanh(math.sqrt(2.0 / math.pi) * (y + 0.044715 * (y * y * y))))


def _spread_groups(m, rep):
    w = m.shape[-1]
    acc = jnp.zeros((m.shape[0], rep.shape[1]), F32)
    for _ in range(3):
        part = m.astype(BF16)
        m = m - part.astype(F32)
        acc = acc + jnp.dot(part, rep, preferred_element_type=F32)
    row_group = lax.broadcasted_iota(jnp.int32, acc.shape, 0) // S5_GROUP
    col_group = lax.broadcasted_iota(jnp.int32, acc.shape, 1) // w
    return jnp.where(row_group == col_group, acc, 0.0)


def _s5_build_weights(cwin_ref, cwout_ref, ckin_ref, rep_p_ref, rep_c_ref, win_ref, wout_ref, tin_ref):
    rep_p = rep_p_ref[...]
    rep_c = rep_c_ref[...]

    def build(j, carry):
        for i in range(S5_CH):
            rows = slice(i * LANES, (i + 1) * LANES)
            blk = jnp.concatenate([_spread_groups(cwin_ref[0, i, j], rep_p),
                                   _spread_groups(cwin_ref[1, i, j], rep_p)], axis=1)
            win_ref[j, rows, :] = blk.astype(BF16)
            blk = jnp.concatenate([_spread_groups(cwout_ref[0, i, j], rep_p),
                                   _spread_groups(cwout_ref[1, i, j], rep_p)], axis=1)
            wout_ref[j, :, rows] = blk.T.astype(BF16)
        lag = [_spread_groups(ckin_ref[dl, j], rep_c).astype(BF16) for dl in range(S5_CH)]
        zero = jnp.zeros((LANES, LANES), BF16)
        for a in range(S5_CH):
            for b in range(S5_CH):
                tin_ref[j, a * LANES:(a + 1) * LANES, b * LANES:(b + 1) * LANES] = lag[b - a] if b >= a else zero
        return carry

    lax.fori_loop(0, S5_SLABS, build, 0)


def _s5_body(x_ref, g_ref, cwin_ref, cwout_ref, ckin_ref, rep_p_ref, rep_c_ref, lamk_ref, d_ref, wglu_ref,
             o_ref, win_ref, wout_ref, tin_ref, utb_ref, st_ref, carry_ref, z_ref, gtb_ref, *, batch):
    nch = S5_LC // S5_CH
    crows = nch * batch
    rows = S5_LC * batch

    @pl.when(pl.program_id(0) == 0)
    def _():
        carry_ref[...] = jnp.zeros_like(carry_ref)
        _s5_build_weights(cwin_ref, cwout_ref, ckin_ref, rep_p_ref, rep_c_ref, win_ref, wout_ref, tin_ref)

    for b in range(batch):
        hb = _rms(x_ref[b], g_ref[...])
        for j in range(S5_SLABS):
            utb_ref[j, pl.ds(b, S5_LC, stride=batch), :] = hb[:, j * LANES:(j + 1) * LANES]

    for j in range(S5_SLABS):
        sl = slice(j * LANES, (j + 1) * LANES)
        u = utb_ref[j]
        u4 = u.reshape(nch, S5_CH, batch, LANES)
        ucat = jnp.concatenate([u4[:, i].reshape(crows, LANES) for i in range(S5_CH)],
                               axis=1).astype(BF16)
        st_ref[j] = jnp.dot(ucat, win_ref[j], preferred_element_type=F32)

        ar = lamk_ref[j, :, :S5_HALF]
        ai = lamk_ref[j, :, S5_HALF:]
        hr = carry_ref[j, :, :S5_HALF]
        hi = carry_ref[j, :, S5_HALF:]
        for c in range(nch):
            rs = pl.ds(c * batch, batch)
            gr = st_ref[j, rs, :S5_HALF]
            gi = st_ref[j, rs, S5_HALF:]
            st_ref[j, rs, :S5_HALF] = hr
            st_ref[j, rs, S5_HALF:] = hi
            hr, hi = ar * hr - ai * hi + gr, ar * hi + ai * hr + gi
        carry_ref[j, :, :S5_HALF] = hr
        carry_ref[j, :, S5_HALF:] = hi

        y4 = (jnp.dot(st_ref[j].astype(BF16), wout_ref[j], preferred_element_type=F32)
              + jnp.dot(ucat, tin_ref[j], preferred_element_type=F32))
        y = jnp.concatenate([y4[:, i * LANES:(i + 1) * LANES].reshape(nch, 1, batch, LANES)
                             for i in range(S5_CH)], axis=1).reshape(rows, LANES)
        y = y + d_ref[:, sl] * u
        z_ref[:, sl] = _gelu_tanh(y).astype(BF16)

    z = z_ref[...]
    for c in range(D_MODEL // MXU_COLS):
        val = jnp.dot(z, wglu_ref[:, c * MXU_COLS:(c + 1) * MXU_COLS], preferred_element_type=F32)
        gate = jnp.dot(z, wglu_ref[:, D_MODEL + c * MXU_COLS:D_MODEL + (c + 1) * MXU_COLS],
                       preferred_element_type=F32)
        glu = val * jax.nn.sigmoid(gate)
        for h in range(MXU_COLS // LANES):
            gtb_ref[c * (MXU_COLS // LANES) + h] = glu[:, h * LANES:(h + 1) * LANES]

    for b in range(batch):
        for j in range(S5_SLABS):
            sl = slice(j * LANES, (j + 1) * LANES)
            o_ref[b, :, sl] = x_ref[b, :, sl] + gtb_ref[j, pl.ds(b, S5_LC, stride=batch), :]


def _s5(x, g, cwin, cwout, ckin, rep_p, rep_c, lamk, d_skip, w_glu):
    b, s, _ = x.shape
    assert b == SUBLANES
    rows = S5_LC * b
    crows = rows // S5_CH
    blk = pl.BlockSpec((b, S5_LC, D_MODEL), lambda i: (0, i, 0))
    return pl.pallas_call(
        functools.partial(_s5_body, batch=b),
        grid=(s // S5_LC,),
        in_specs=[
            blk,
            _const_spec((1, D_MODEL)),
            _const_spec(cwin.shape),
            _const_spec(cwout.shape),
            _const_spec(ckin.shape),
            _const_spec(rep_p.shape),
            _const_spec(rep_c.shape),
            _const_spec((S5_SLABS, SUBLANES, 2 * S5_HALF)),
            _const_spec((1, D_MODEL)),
            _const_spec((D_MODEL, 2 * D_MODEL)),
        ],
        out_specs=blk,
        out_shape=jax.ShapeDtypeStruct(x.shape, F32),
        scratch_shapes=[
            pltpu.VMEM((S5_SLABS, S5_CH * LANES, 2 * S5_HALF), BF16),
            pltpu.VMEM((S5_SLABS, 2 * S5_HALF, S5_CH * LANES), BF16),
            pltpu.VMEM((S5_SLABS, S5_CH * LANES, S5_CH * LANES), BF16),
            pltpu.VMEM((S5_SLABS, rows, LANES), F32),
            pltpu.VMEM((S5_SLABS, crows, 2 * S5_HALF), F32),
            pltpu.VMEM((S5_SLABS, SUBLANES, 2 * S5_HALF), F32),
            pltpu.VMEM((rows, D_MODEL), BF16),
            pltpu.VMEM((S5_SLABS, rows, LANES), F32),
        ],
        compiler_params=pltpu.CompilerParams(
            dimension_semantics=("arbitrary",), vmem_limit_bytes=VMEM_LIMIT),
        name="s5_glu",
    )(x, g, cwin, cwout, ckin, rep_p, rep_c, lamk, d_skip, w_glu)


def _rotary_tables(positions):
    half = ROT_DIM // 2
    inv_freq = ROPE_THETA ** (-jnp.arange(0, ROT_DIM, 2, dtype=F32) / ROT_DIM)
    ang = inv_freq[:, None, None] * positions.astype(F32)[None]
    cs = jnp.concatenate([jnp.cos(ang), jnp.sin(ang)], axis=0).transpose(1, 2, 0)
    lane = jnp.arange(LANES) % HEAD_DIM
    freq = jnp.arange(half)[:, None]
    first = (lane[None, :] == freq).astype(F32)
    second = (lane[None, :] == freq + half).astype(F32)
    zeros = jnp.zeros_like(first)
    spread = jnp.concatenate([
        jnp.concatenate([first + second, zeros], axis=1),
        jnp.concatenate([zeros, second - first], axis=1),
    ], axis=0).astype(BF16)
    return cs, spread


def _s5_operands(lamk_re, lamk_im, win_re, win_im, wout_re, wout_im, kin_t):
    gl = LANES // S5_GROUP
    by_slab = lambda m: m.reshape(m.shape[:-3] + (S5_SLABS, LANES, m.shape[-1]))
    cwin = by_slab(jnp.stack([win_re, win_im]))
    cwout = by_slab(jnp.stack([wout_re, wout_im]))
    ckin = by_slab(kin_t)
    rep_p = jnp.tile(jnp.eye(S5_STATE, dtype=BF16), (1, gl))
    rep_c = jnp.tile(jnp.eye(S5_GROUP, dtype=BF16), (1, gl))
    lamk = jnp.concatenate([lamk_re.reshape(S5_SLABS, 1, S5_HALF), lamk_im.reshape(S5_SLABS, 1, S5_HALF)],
                           axis=-1)
    lamk = jnp.broadcast_to(lamk, (S5_SLABS, SUBLANES, 2 * S5_HALF))
    return cwin, cwout, ckin, rep_p, rep_c, lamk


def kernel(x, positions, ffn_norm, ffn_w_gate, ffn_w_up, ffn_w_down, mix_norm, ev_w_in, ev_q_norm, ev_k_norm, ev_pool_w, ev_pool_scale, ev_w_out, s5_a_re, s5_a_im, s5_log_dt, s5_b_re, s5_b_im, s5_c_re, s5_c_im, s5_d, s5_w_glu):
    b, s, d = x.shape
    t = b * s
    depth = ffn_norm.shape[0]
    cs, spread = _rotary_tables(positions)
    ones_bd = jnp.kron(jnp.eye(D_ATTN // HEAD_DIM, dtype=F32),
                       jnp.ones((HEAD_DIM, HEAD_DIM), F32)).astype(BF16)
    wg, wu, wd = ffn_w_gate.astype(BF16), ffn_w_up.astype(BF16), ffn_w_down.astype(BF16)
    w_in, pool_w, w_out = ev_w_in.astype(BF16), ev_pool_w.astype(BF16), ev_w_out.astype(BF16)

    def ffn(xx, layer, half, mixer=None):
        return _ffn(xx.reshape(t, d), ffn_norm[layer, half][None, :],
                    wg, wu, wd, layer, half, mixer).reshape(b, s, d)

    for layer in range(depth):
        x = ffn(x, layer, 0)
        j = layer // 2
        g = mix_norm[layer][None, :]
        mixer = None
        if layer % 2 == 0:
            q, k, v, pool = _evproj(
                x, g, w_in, j,
                jnp.tile(ev_q_norm[j], D_ATTN // HEAD_DIM)[None, :],
                jnp.tile(ev_k_norm[j], D_ATTN // HEAD_DIM)[None, :],
                ones_bd, cs, spread, pool_w, ev_pool_scale[j][None, :])
            mixer = (_attn(q, k, v), pool.reshape(t, D_POOL), w_out, j)
        else:
            prep = _s5prep(
                s5_a_re[j], s5_a_im[j], s5_log_dt[j][:, None],
                s5_b_re[j].transpose(0, 2, 1), s5_b_im[j].transpose(0, 2, 1), s5_c_re[j], s5_c_im[j])
            x = _s5(x, g, *_s5_operands(*prep), s5_d[j][None, :], s5_w_glu[j].astype(BF16))
        x = ffn(x, layer, 1, mixer)
    return x
```

```python
import functools
import math

import jax
import jax.numpy as jnp
from jax import lax
from jax.experimental import pallas as pl
from jax.experimental.pallas import tpu as pltpu

F32 = jnp.float32
BF16 = jnp.bfloat16

D_MODEL = 1024
D_ATTN = 512
HEAD_DIM = 64
ROT_DIM = 16
ROPE_THETA = 500000.0
DILATIONS = (1, 4, 16)
N_BACK = 128
D_POOL = 512
POOL_WINDOWS = (2, 4, 8, 16)
POOL_C = 128
S5_GROUP = 16
S5_GROUPS = 64
S5_STATE = 64
D_FF = 2816
EPS = 1e-6

LANES = 128
SUBLANES = 8
MXU_COLS = 256
VMEM_LIMIT = 56 * 1024 * 1024

NEG_BIG = -1e30


def _rms(x, g):
    return x * lax.rsqrt(jnp.mean(x * x, axis=-1, keepdims=True) + EPS) * g


def _const_spec(shape):
    nd = len(shape)
    return pl.BlockSpec(shape, lambda *_: (0,) * nd, pipeline_mode=pl.Buffered(1))


FFN_TM = 512
ATT_RES = 4


def _ffn_body(*refs, mixer):
    if mixer:
        x_ref, at_ref, pool_ref, wa_ref, wp_ref, g_ref, wg_ref, wu_ref, wd_ref, o_ref, a_ref, nat_ref = refs
        sub = FFN_TM // ATT_RES
        for j in range(D_ATTN // LANES):
            for r in range(ATT_RES):
                nat_ref[j, pl.ds(r, sub, stride=ATT_RES), :] = at_ref[r, :, j * LANES:(j + 1) * LANES]
        attn = jnp.concatenate([nat_ref[j] for j in range(D_ATTN // LANES)], axis=1)
        x = (x_ref[...]
             + jnp.dot(attn.astype(BF16), wa_ref[...], preferred_element_type=F32)
             + jnp.dot(pool_ref[...].astype(BF16), wp_ref[...], preferred_element_type=F32))
    else:
        x_ref, g_ref, wg_ref, wu_ref, wd_ref, o_ref, a_ref = refs
        x = x_ref[...]
    h = _rms(x, g_ref[...]).astype(BF16)
    for c in range(D_FF // MXU_COLS):
        sl = slice(c * MXU_COLS, (c + 1) * MXU_COLS)
        gate = jnp.dot(h, wg_ref[:, sl], preferred_element_type=F32)
        up = jnp.dot(h, wu_ref[:, sl], preferred_element_type=F32)
        a_ref[:, sl] = (gate * jax.nn.sigmoid(gate) * up).astype(BF16)
    o_ref[...] = x + 0.5 * jnp.dot(a_ref[...], wd_ref[...], preferred_element_type=F32)


def _ffn(x2d, g, wg, wu, wd, layer, half, mixer=None):
    t = x2d.shape[0]
    pick = lambda i: (layer, half, 0, 0)
    single = pl.Buffered(1)
    row = lambda i: (i, 0)
    in_specs = [pl.BlockSpec((FFN_TM, D_MODEL), row)]
    args = [x2d]
    scratch = [pltpu.VMEM((FFN_TM, D_FF), BF16)]
    if mixer is not None:
        attn, pool, w_out, j = mixer
        per_seq = attn.shape[2] * ATT_RES // FFN_TM
        in_specs += [
            pl.BlockSpec((None, ATT_RES, FFN_TM // ATT_RES, D_ATTN),
                         lambda i: (i // per_seq, 0, i % per_seq, 0)),
            pl.BlockSpec((FFN_TM, D_POOL), row),
            pl.BlockSpec((None, D_ATTN, D_MODEL), lambda i: (j, 0, 0), pipeline_mode=single),
            pl.BlockSpec((None, D_POOL, D_MODEL), lambda i: (j, 1, 0), pipeline_mode=single),
        ]
        args += [attn, pool, w_out, w_out]
        scratch.append(pltpu.VMEM((D_ATTN // LANES, FFN_TM, LANES), F32))
    in_specs += [
        _const_spec((1, D_MODEL)),
        pl.BlockSpec((None, None, D_MODEL, D_FF), pick, pipeline_mode=single),
        pl.BlockSpec((None, None, D_MODEL, D_FF), pick, pipeline_mode=single),
        pl.BlockSpec((None, None, D_FF, D_MODEL), pick, pipeline_mode=single),
    ]
    args += [g, wg, wu, wd]
    return pl.pallas_call(
        functools.partial(_ffn_body, mixer=mixer is not None),
        grid=(t // FFN_TM,),
        in_specs=in_specs,
        out_specs=pl.BlockSpec((FFN_TM, D_MODEL), row),
        out_shape=jax.ShapeDtypeStruct((t, D_MODEL), F32),
        scratch_shapes=scratch,
        compiler_params=pltpu.CompilerParams(
            dimension_semantics=("arbitrary",), vmem_limit_bytes=VMEM_LIMIT),
        name="ffn_mix" if mixer is not None else "ffn",
    )(*args)


EV_TM = 1024
POOL_HALO = 16


def _head_norm_rot(t, gn, ones_bd, c128, s128):
    ssq = jnp.dot((t * t).astype(BF16), ones_bd, preferred_element_type=F32)
    y = t * lax.rsqrt(ssq * (1.0 / HEAD_DIM) + EPS) * gn
    lane = lax.broadcasted_iota(jnp.int32, (1, LANES), 1) % HEAD_DIM
    outs = []
    for j in range(D_ATTN // LANES):
        yj = y[:, j * LANES:(j + 1) * LANES]
        swapped = jnp.where(lane < ROT_DIM // 2,
                            pltpu.roll(yj, LANES - ROT_DIM // 2, 1),
                            pltpu.roll(yj, ROT_DIM // 2, 1))
        outs.append(yj * c128 + swapped * s128)
    return outs


def _evproj_body(x_ref, g_ref, w_ref, qn_ref, kn_ref, ones_ref, cs_ref, e_ref,
                 pw_ref, ps_ref, q_ref, k_ref, v_ref, p_ref, carry_ref, stage_ref):
    it = pl.program_id(1)

    @pl.when(it == 0)
    def _():
        carry_ref[...] = jnp.zeros_like(carry_ref)

    x = x_ref[...]
    h = _rms(x, g_ref[...]).astype(BF16)
    proj = jnp.dot(h, w_ref[...], preferred_element_type=F32)
    ones_bd = ones_ref[...]
    cs = cs_ref[...]
    cs_hi = cs.astype(BF16)
    cs_lo = (cs - cs_hi.astype(F32)).astype(BF16)
    spread = (jnp.dot(cs_hi, e_ref[...], preferred_element_type=F32)
              + jnp.dot(cs_lo, e_ref[...], preferred_element_type=F32))
    lane = lax.broadcasted_iota(jnp.int32, (1, LANES), 1) % HEAD_DIM
    c128 = spread[:, :LANES] + jnp.where(lane < ROT_DIM, 0.0, 1.0)
    s128 = spread[:, LANES:]
    qs = _head_norm_rot(proj[:, :D_ATTN], qn_ref[...], ones_bd, c128, s128)
    ks = _head_norm_rot(proj[:, D_ATTN:2 * D_ATTN], kn_ref[...], ones_bd, c128, s128)
    sub = EV_TM // ATT_RES
    for a, (o_ref, slabs) in enumerate((
            (q_ref, qs), (k_ref, ks),
            (v_ref, [proj[:, 2 * D_ATTN + j * LANES:2 * D_ATTN + (j + 1) * LANES]
                     for j in range(D_ATTN // LANES)]))):
        for j in range(D_ATTN // LANES):
            stage_ref[a * (D_ATTN // LANES) + j] = slabs[j]
        for j in range(D_ATTN // LANES):
            for r in range(ATT_RES):
                o_ref[r, :, j * LANES:(j + 1) * LANES] = stage_ref[
                    a * (D_ATTN // LANES) + j, pl.ds(r, sub, stride=ATT_RES), :]

    p = proj[:, 3 * D_ATTN:]
    pe = jnp.concatenate([carry_ref[...], p], axis=0)
    carry_ref[...] = p[EV_TM - POOL_HALO:, :]
    pos1 = (it * EV_TM + 1 + lax.broadcasted_iota(jnp.int32, (EV_TM, 1), 0)).astype(F32)
    for g, w in enumerate(POOL_WINDOWS):
        sl = slice(g * POOL_C, (g + 1) * POOL_C)
        s = pe[:, sl]
        shift = 1
        while shift < w:
            s = s + pltpu.roll(s, shift, 0)
            shift *= 2
        count = jnp.minimum(pos1, float(w))
        pooled = s[POOL_HALO:, :] / count - p[:, sl]
        y = jnp.dot(pooled.astype(BF16), pw_ref[g], preferred_element_type=F32)
        p_ref[:, sl] = y * ps_ref[:, sl]


def _evproj(x, g, w_in, j, qn, kn, ones_bd, cs, spread, pool_w, pool_scale):
    b, s, _ = x.shape
    tok = lambda bi, i: (bi, i, 0)
    out = jax.ShapeDtypeStruct((b, s, D_ATTN), F32)
    res_out = jax.ShapeDtypeStruct((b, ATT_RES, s // ATT_RES, D_ATTN), F32)
    res_spec = pl.BlockSpec((None, ATT_RES, EV_TM // ATT_RES, D_ATTN), lambda bi, i: (bi, 0, i, 0))
    single = pl.Buffered(1)
    return pl.pallas_call(
        _evproj_body,
        grid=(b, s // EV_TM),
        in_specs=[
            pl.BlockSpec((None, EV_TM, D_MODEL), tok),
            _const_spec((1, D_MODEL)),
            pl.BlockSpec((None, D_MODEL, 4 * D_ATTN), lambda bi, i: (j, 0, 0), pipeline_mode=single),
            _const_spec((1, D_ATTN)),
            _const_spec((1, D_ATTN)),
            _const_spec((D_ATTN, D_ATTN)),
            pl.BlockSpec((None, EV_TM, ROT_DIM), tok),
            _const_spec((ROT_DIM, 2 * LANES)),
            pl.BlockSpec((None, len(POOL_WINDOWS), POOL_C, POOL_C), lambda bi, i: (j, 0, 0, 0),
                         pipeline_mode=single),
            _const_spec((1, D_POOL)),
        ],
        out_specs=[res_spec] * 3 + [pl.BlockSpec((None, EV_TM, D_ATTN), tok)],
        out_shape=[res_out] * 3 + [out],
        scratch_shapes=[pltpu.VMEM((POOL_HALO, D_POOL), F32),
                        pltpu.VMEM((3 * D_ATTN // LANES, EV_TM, LANES), F32)],
        compiler_params=pltpu.CompilerParams(
            dimension_semantics=("arbitrary", "arbitrary"), vmem_limit_bytes=VMEM_LIMIT),
        name="evproj",
    )(x, g, w_in, qn, kn, ones_bd, cs, spread, pool_w, pool_scale)


ATT_BLK = N_BACK


ATT_UNROLL = 32
ATT_QSCALE = HEAD_DIM ** -0.5 * math.log2(math.e)


def _attn_body(q_ref, k_ref, v_ref, o_ref, bias_ref, *stats, seq):
    lane = lax.broadcasted_iota(jnp.int32, (1, LANES), 1)
    head0 = lane < HEAD_DIM
    units = seq // ATT_BLK
    sub = seq // ATT_RES
    qc = ATT_BLK // ATT_RES
    kc = 2 * qc

    qi = lax.broadcasted_iota(jnp.int32, (ATT_BLK, 2 * ATT_BLK), 0)
    kj = lax.broadcasted_iota(jnp.int32, (ATT_BLK, 2 * ATT_BLK), 1)
    qpos = (qi % qc) * ATT_RES + qi // qc
    kpos = (kj % kc) * ATT_RES + kj // kc
    for idx, (dq, off) in enumerate(((qi - kj, 0), (qi - kj, ATT_BLK),
                                     (qpos - kpos, 0), (qpos - kpos, ATT_BLK))):
        delta = dq + off
        bias_ref[idx] = jnp.where((delta >= 0) & (delta <= N_BACK), 0.0, NEG_BIG)

    ones = jnp.ones((2 * ATT_BLK, LANES), BF16)

    def attend(q2, k2, v2, bias):
        q2 = q2 * ATT_QSCALE
        qq = jnp.concatenate([jnp.where(head0, q2, 0.0), jnp.where(head0, 0.0, q2)],
                             axis=0).astype(BF16)
        vaug = jnp.concatenate([v2.astype(BF16), ones], axis=1)
        sc = lax.dot_general(qq, k2.astype(BF16), (((1,), (1,)), ((), ())),
                             preferred_element_type=F32)
        sc = sc + jnp.concatenate([bias, bias], axis=0)
        m = jnp.max(sc, axis=-1, keepdims=True)
        p = jnp.exp2(sc - m).astype(BF16)
        res = jnp.dot(p, vaug, preferred_element_type=F32)
        return (jnp.where(head0, res[:ATT_BLK, :LANES], res[ATT_BLK:, :LANES]),
                jnp.where(head0, res[:ATT_BLK, LANES:], res[ATT_BLK:, LANES:]),
                jnp.where(head0, m[:ATT_BLK], m[ATT_BLK:]))

    def unit_consecutive(nb, refs):
        q0 = pl.multiple_of(nb * qc, qc)
        k0 = pl.multiple_of(jnp.maximum(nb - 1, 0) * qc, qc)
        q2 = jnp.concatenate([q_ref[c, pl.ds(q0, qc), :] for c in range(ATT_RES)], axis=0)
        k2 = jnp.concatenate([k_ref[c, pl.ds(k0, kc), :] for c in range(ATT_RES)], axis=0)
        v2 = jnp.concatenate([v_ref[c, pl.ds(k0, kc), :] for c in range(ATT_RES)], axis=0)
        outs = attend(q2, k2, v2, bias_ref[2 + jnp.minimum(nb, 1)])
        for ref, val in zip(refs, outs):
            for c in range(ATT_RES):
                ref[c, pl.ds(q0, qc), :] = val[c * qc:(c + 1) * qc]

    def unit_strided(u, refs, step):
        per_res = units // ATT_RES
        r = u // per_res
        w = u % per_res
        nblk = per_res // step
        rr = w // nblk
        nb = w % nblk
        kb = jnp.maximum(nb - 1, 0)

        def rows(blk, n):
            start = rr + step * ATT_BLK * blk
            return pl.ds(start, n) if step == 1 else pl.ds(start, n, stride=step)

        qrows = rows(nb, ATT_BLK)
        krows = rows(kb, 2 * ATT_BLK)
        outs = attend(q_ref[r, qrows, :], k_ref[r, krows, :], v_ref[r, krows, :],
                      bias_ref[jnp.minimum(nb, 1)])
        for ref, val in zip(refs, outs):
            ref[r, qrows, :] = val

    for pi, d in enumerate(DILATIONS):
        refs = stats[3 * pi:3 * pi + 3]

        def group(it, carry, d=d, refs=refs):
            for c in range(ATT_UNROLL):
                u = it * ATT_UNROLL + c
                if d == 1:
                    unit_consecutive(u, refs)
                else:
                    unit_strided(u, refs, d // ATT_RES)
            return carry

        lax.fori_loop(0, units // ATT_UNROLL, group, 0)

    def finish(i, carry):
        per_res = sub // ATT_BLK
        r = i // per_res
        rs = pl.ds(pl.multiple_of((i % per_res) * ATT_BLK, ATT_BLK), ATT_BLK)
        ms = [stats[3 * pi + 2][r, rs, :] for pi in range(len(DILATIONS))]
        m_tot = functools.reduce(jnp.maximum, ms)
        ws = [jnp.exp2(m - m_tot) for m in ms]
        num = sum(w * stats[3 * pi][r, rs, :] for pi, w in enumerate(ws))
        den = sum(w * stats[3 * pi + 1][r, rs, :] for pi, w in enumerate(ws))
        o_ref[r, rs, :] = num / den
        return carry

    lax.fori_loop(0, units, finish, 0, unroll=4)


def _attn(q, k, v):
    b, res, sub, _ = q.shape
    spec = pl.BlockSpec((None, res, sub, LANES), lambda bi, hp: (bi, 0, 0, hp))
    return pl.pallas_call(
        functools.partial(_attn_body, seq=res * sub),
        grid=(b, D_ATTN // LANES),
        in_specs=[spec] * 3,
        out_specs=spec,
        out_shape=jax.ShapeDtypeStruct(q.shape, F32),
        scratch_shapes=([pltpu.VMEM((4, ATT_BLK, 2 * ATT_BLK), F32)]
                        + [pltpu.VMEM((res, sub, LANES), F32)] * (3 * len(DILATIONS))),
        compiler_params=pltpu.CompilerParams(
            dimension_semantics=("arbitrary", "arbitrary"), vmem_limit_bytes=VMEM_LIMIT),
        name="dilated_attn",
    )(q, k, v)


S5_CH = 4


def _cmul(ar, ai, br, bi):
    return ar * br - ai * bi, ar * bi + ai * br


def _s5prep_body(are_ref, aim_ref, ldt_ref, brt_ref, bit_ref, cre_ref, cim_ref,
                 lamk_re_ref, lamk_im_ref, win_re_ref, win_im_ref, wout_re_ref, wout_im_ref, kin_ref):
    lam_re = jnp.minimum(are_ref[...], -1e-4)
    lam_im = aim_ref[...]
    dt = jnp.exp(ldt_ref[...])
    mag = jnp.exp(lam_re * dt)
    lbr = mag * jnp.cos(lam_im * dt)
    lbi = mag * jnp.sin(lam_im * dt)
    den = lam_re * lam_re + lam_im * lam_im
    nre = lbr - 1.0
    cr = (nre * lam_re + lbi * lam_im) / den
    ci = (lbi * lam_re - nre * lam_im) / den
    bbr, bbi = _cmul(cr[:, None, :], ci[:, None, :], brt_ref[...], bit_ref[...])
    c_re = cre_ref[...]
    c_im = cim_ref[...]

    powers = [(jnp.ones_like(lbr), jnp.zeros_like(lbr))]
    for _ in range(S5_CH):
        powers.append(_cmul(powers[-1][0], powers[-1][1], lbr, lbi))
    lamk_re_ref[...], lamk_im_ref[...] = powers[S5_CH]

    contract_p = (((2,), (2,)), ((0,), (0,)))
    for i in range(S5_CH):
        pr, pi = powers[S5_CH - 1 - i]
        win_re_ref[i], win_im_ref[i] = _cmul(pr[:, None, :], pi[:, None, :], bbr, bbi)
        qr, qi = powers[i + 1]
        er, ei = _cmul(c_re, c_im, qr[:, None, :], qi[:, None, :])
        wout_re_ref[i] = er
        wout_im_ref[i] = -ei
        er, ei = _cmul(c_re, c_im, powers[i][0][:, None, :], powers[i][1][:, None, :])
        kin_ref[i] = (lax.dot_general(bbr, er, contract_p, precision=lax.Precision.HIGHEST,
                                      preferred_element_type=F32)
                      - lax.dot_general(bbi, ei, contract_p, precision=lax.Precision.HIGHEST,
                                        preferred_element_type=F32))


def _s5prep(a_re, a_im, log_dt, b_re_t, b_im_t, c_re, c_im):
    gp = jax.ShapeDtypeStruct((S5_GROUPS, S5_STATE), F32)
    kgcp = jax.ShapeDtypeStruct((S5_CH, S5_GROUPS, S5_GROUP, S5_STATE), F32)
    kgcc = jax.ShapeDtypeStruct((S5_CH, S5_GROUPS, S5_GROUP, S5_GROUP), F32)
    return pl.pallas_call(
        _s5prep_body,
        out_shape=[gp, gp, kgcp, kgcp, kgcp, kgcp, kgcc],
        name="s5prep",
    )(a_re, a_im, log_dt, b_re_t, b_im_t, c_re, c_im)


S5_LC = 64
S5_SLABS = D_MODEL // LANES
S5_HALF = LANES // S5_GROUP * S5_STATE


def _gelu_tanh(y):
    return 0.5 * y * (1.0 + jnp.tanh(math.sqrt(2.0 / math.pi) * (y + 0.044715 * (y * y * y))))


def _spread_groups(m, rep):
    w = m.shape[-1]
    acc = jnp.zeros((m.shape[0], rep.shape[1]), F32)
    for _ in range(3):
        part = m.astype(BF16)
        m = m - part.astype(F32)
        acc = acc + jnp.dot(part, rep, preferred_element_type=F32)
    row_group = lax.broadcasted_iota(jnp.int32, acc.shape, 0) // S5_GROUP
    col_group = lax.broadcasted_iota(jnp.int32, acc.shape, 1) // w
    return jnp.where(row_group == col_group, acc, 0.0)


def _s5_build_weights(cwin_ref, cwout_ref, ckin_ref, rep_p_ref, rep_c_ref, win_ref, wout_ref, tin_ref):
    rep_p = rep_p_ref[...]
    rep_c = rep_c_ref[...]

    def build(j, carry):
        for i in range(S5_CH):
            rows = slice(i * LANES, (i + 1) * LANES)
            blk = jnp.concatenate([_spread_groups(cwin_ref[0, i, j], rep_p),
                                   _spread_groups(cwin_ref[1, i, j], rep_p)], axis=1)
            win_ref[j, rows, :] = blk.astype(BF16)
            blk = jnp.concatenate([_spread_groups(cwout_ref[0, i, j], rep_p),
                                   _spread_groups(cwout_ref[1, i, j], rep_p)], axis=1)
            wout_ref[j, :, rows] = blk.T.astype(BF16)
        lag = [_spread_groups(ckin_ref[dl, j], rep_c).astype(BF16) for dl in range(S5_CH)]
        zero = jnp.zeros((LANES, LANES), BF16)
        for a in range(S5_CH):
            for b in range(S5_CH):
                tin_ref[j, a * LANES:(a + 1) * LANES, b * LANES:(b + 1) * LANES] = lag[b - a] if b >= a else zero
        return carry

    lax.fori_loop(0, S5_SLABS, build, 0)


def _s5_body(x_ref, g_ref, cwin_ref, cwout_ref, ckin_ref, rep_p_ref, rep_c_ref, lamk_ref, d_ref, wglu_ref,
             o_ref, win_ref, wout_ref, tin_ref, utb_ref, st_ref, carry_ref, z_ref, gtb_ref, *, batch):
    nch = S5_LC // S5_CH
    crows = nch * batch
    rows = S5_LC * batch

    @pl.when(pl.program_id(0) == 0)
    def _():
        carry_ref[...] = jnp.zeros_like(carry_ref)
        _s5_build_weights(cwin_ref, cwout_ref, ckin_ref, rep_p_ref, rep_c_ref, win_ref, wout_ref, tin_ref)

    for b in range(batch):
        hb = _rms(x_ref[b], g_ref[...])
        for j in range(S5_SLABS):
            utb_ref[j, pl.ds(b, S5_LC, stride=batch), :] = hb[:, j * LANES:(j + 1) * LANES]

    for j in range(S5_SLABS):
        sl = slice(j * LANES, (j + 1) * LANES)
        u = utb_ref[j]
        u4 = u.reshape(nch, S5_CH, batch, LANES)
        ucat = jnp.concatenate([u4[:, i].reshape(crows, LANES) for i in range(S5_CH)],
                               axis=1).astype(BF16)
        st_ref[j] = jnp.dot(ucat, win_ref[j], preferred_element_type=F32)

        ar = lamk_ref[j, :, :S5_HALF]
        ai = lamk_ref[j, :, S5_HALF:]
        hr = carry_ref[j, :, :S5_HALF]
        hi = carry_ref[j, :, S5_HALF:]
        for c in range(nch):
            rs = pl.ds(c * batch, batch)
            gr = st_ref[j, rs, :S5_HALF]
            gi = st_ref[j, rs, S5_HALF:]
            st_ref[j, rs, :S5_HALF] = hr
            st_ref[j, rs, S5_HALF:] = hi
            hr, hi = ar * hr - ai * hi + gr, ar * hi + ai * hr + gi
        carry_ref[j, :, :S5_HALF] = hr
        carry_ref[j, :, S5_HALF:] = hi

        y4 = (jnp.dot(st_ref[j].astype(BF16), wout_ref[j], preferred_element_type=F32)
              + jnp.dot(ucat, tin_ref[j], preferred_element_type=F32))
        y = jnp.concatenate([y4[:, i * LANES:(i + 1) * LANES].reshape(nch, 1, batch, LANES)
                             for i in range(S5_CH)], axis=1).reshape(rows, LANES)
        y = y + d_ref[:, sl] * u
        z_ref[:, sl] = _gelu_tanh(y).astype(BF16)

    z = z_ref[...]
    for c in range(D_MODEL // MXU_COLS):
        val = jnp.dot(z, wglu_ref[:, c * MXU_COLS:(c + 1) * MXU_COLS], preferred_element_type=F32)
        gate = jnp.dot(z, wglu_ref[:, D_MODEL + c * MXU_COLS:D_MODEL + (c + 1) * MXU_COLS],
                       preferred_element_type=F32)
        glu = val * jax.nn.sigmoid(gate)
        for h in range(MXU_COLS // LANES):
            gtb_ref[c * (MXU_COLS // LANES) + h] = glu[:, h * LANES:(h + 1) * LANES]

    for b in range(batch):
        for j in range(S5_SLABS):
            sl = slice(j * LANES, (j + 1) * LANES)
            o_ref[b, :, sl] = x_ref[b, :, sl] + gtb_ref[j, pl.ds(b, S5_LC, stride=batch), :]


def _s5(x, g, cwin, cwout, ckin, rep_p, rep_c, lamk, d_skip, w_glu):
    b, s, _ = x.shape
    assert b == SUBLANES
    rows = S5_LC * b
    crows = rows // S5_CH
    blk = pl.BlockSpec((b, S5_LC, D_MODEL), lambda i: (0, i, 0))
    return pl.pallas_call(
        functools.partial(_s5_body, batch=b),
        grid=(s // S5_LC,),
        in_specs=[
            blk,
            _const_spec((1, D_MODEL)),
            _const_spec(cwin.shape),
            _const_spec(cwout.shape),
            _const_spec(ckin.shape),
            _const_spec(rep_p.shape),
            _const_spec(rep_c.shape),
            _const_spec((S5_SLABS, SUBLANES, 2 * S5_HALF)),
            _const_spec((1, D_MODEL)),
            _const_spec((D_MODEL, 2 * D_MODEL)),
        ],
        out_specs=blk,
        out_shape=jax.ShapeDtypeStruct(x.shape, F32),
        scratch_shapes=[
            pltpu.VMEM((S5_SLABS, S5_CH * LANES, 2 * S5_HALF), BF16),
            pltpu.VMEM((S5_SLABS, 2 * S5_HALF, S5_CH * LANES), BF16),
            pltpu.VMEM((S5_SLABS, S5_CH * LANES, S5_CH * LANES), BF16),
            pltpu.VMEM((S5_SLABS, rows, LANES), F32),
            pltpu.VMEM((S5_SLABS, crows, 2 * S5_HALF), F32),
            pltpu.VMEM((S5_SLABS, SUBLANES, 2 * S5_HALF), F32),
            pltpu.VMEM((rows, D_MODEL), BF16),
            pltpu.VMEM((S5_SLABS, rows, LANES), F32),
        ],
        compiler_params=pltpu.CompilerParams(
            dimension_semantics=("arbitrary",), vmem_limit_bytes=VMEM_LIMIT),
        name="s5_glu",
    )(x, g, cwin, cwout, ckin, rep_p, rep_c, lamk, d_skip, w_glu)


def _rotary_tables(positions):
    half = ROT_DIM // 2
    inv_freq = ROPE_THETA ** (-jnp.arange(0, ROT_DIM, 2, dtype=F32) / ROT_DIM)
    ang = inv_freq[:, None, None] * positions.astype(F32)[None]
    cs = jnp.concatenate([jnp.cos(ang), jnp.sin(ang)], axis=0).transpose(1, 2, 0)
    lane = jnp.arange(LANES) % HEAD_DIM
    freq = jnp.arange(half)[:, None]
    first = (lane[None, :] == freq).astype(F32)
    second = (lane[None, :] == freq + half).astype(F32)
    zeros = jnp.zeros_like(first)
    spread = jnp.concatenate([
        jnp.concatenate([first + second, zeros], axis=1),
        jnp.concatenate([zeros, second - first], axis=1),
    ], axis=0).astype(BF16)
    return cs, spread


def _s5_operands(lamk_re, lamk_im, win_re, win_im, wout_re, wout_im, kin_t):
    gl = LANES // S5_GROUP
    by_slab = lambda m: m.reshape(m.shape[:-3] + (S5_SLABS, LANES, m.shape[-1]))
    cwin = by_slab(jnp.stack([win_re, win_im]))
    cwout = by_slab(jnp.stack([wout_re, wout_im]))
    ckin = by_slab(kin_t)
    rep_p = jnp.tile(jnp.eye(S5_STATE, dtype=BF16), (1, gl))
    rep_c = jnp.tile(jnp.eye(S5_GROUP, dtype=BF16), (1, gl))
    lamk = jnp.concatenate([lamk_re.reshape(S5_SLABS, 1, S5_HALF), lamk_im.reshape(S5_SLABS, 1, S5_HALF)],
                           axis=-1)
    lamk = jnp.broadcast_to(lamk, (S5_SLABS, SUBLANES, 2 * S5_HALF))
    return cwin, cwout, ckin, rep_p, rep_c, lamk


def kernel(x, positions, ffn_norm, ffn_w_gate, ffn_w_up, ffn_w_down, mix_norm, ev_w_in, ev_q_norm, ev_k_norm, ev_pool_w, ev_pool_scale, ev_w_out, s5_a_re, s5_a_im, s5_log_dt, s5_b_re, s5_b_im, s5_c_re, s5_c_im, s5_d, s5_w_glu):
    b, s, d = x.shape
    t = b * s
    depth = ffn_norm.shape[0]
    cs, spread = _rotary_tables(positions)
    ones_bd = jnp.kron(jnp.eye(D_ATTN // HEAD_DIM, dtype=F32),
                       jnp.ones((HEAD_DIM, HEAD_DIM), F32)).astype(BF16)
    wg, wu, wd = ffn_w_gate.astype(BF16), ffn_w_up.astype(BF16), ffn_w_down.astype(BF16)
    w_in, pool_w, w_out = ev_w_in.astype(BF16), ev_pool_w.astype(BF16), ev_w_out.astype(BF16)

    def ffn(xx, layer, half, mixer=None):
        return _ffn(xx.reshape(t, d), ffn_norm[layer, half][None, :],
                    wg, wu, wd, layer, half, mixer).reshape(b, s, d)

    for layer in range(depth):
        x = ffn(x, layer, 0)
        j = layer // 2
        g = mix_norm[layer][None, :]
        mixer = None
        if layer % 2 == 0:
            q, k, v, pool = _evproj(
                x, g, w_in, j,
                jnp.tile(ev_q_norm[j], D_ATTN // HEAD_DIM)[None, :],
                jnp.tile(ev_k_norm[j], D_ATTN // HEAD_DIM)[None, :],
                ones_bd, cs, spread, pool_w, ev_pool_scale[j][None, :])
            mixer = (_attn(q, k, v), pool.reshape(t, D_POOL), w_out, j)
        else:
            prep = _s5prep(
                s5_a_re[j], s5_a_im[j], s5_log_dt[j][:, None],
                s5_b_re[j].transpose(0, 2, 1), s5_b_im[j].transpose(0, 2, 1), s5_c_re[j], s5_c_im[j])
            x = _s5(x, g, *_s5_operands(*prep), s5_d[j][None, :], s5_w_glu[j].astype(BF16))
        x = ffn(x, layer, 1, mixer)
    return x
```

```python
import functools
import math

import jax
import jax.numpy as jnp
from jax import lax
from jax.experimental import pallas as pl
from jax.experimental.pallas import tpu as pltpu

F32 = jnp.float32
BF16 = jnp.bfloat16

D_MODEL = 1024
D_ATTN = 512
HEAD_DIM = 64
ROT_DIM = 16
ROPE_THETA = 500000.0
DILATIONS = (1, 4, 16)
N_BACK = 128
D_POOL = 512
POOL_WINDOWS = (2, 4, 8, 16)
POOL_C = 128
S5_GROUP = 16
S5_GROUPS = 64
S5_STATE = 64
D_FF = 2816
EPS = 1e-6

LANES = 128
SUBLANES = 8
MXU_COLS = 256
VMEM_LIMIT = 56 * 1024 * 1024

NEG_BIG = -1e30


def _rms(x, g):
    return x * lax.rsqrt(jnp.mean(x * x, axis=-1, keepdims=True) + EPS) * g


def _const_spec(shape):
    nd = len(shape)
    return pl.BlockSpec(shape, lambda *_: (0,) * nd, pipeline_mode=pl.Buffered(1))


FFN_TM = 1024
ATT_RES = 4


def _ffn_body(*refs, mixer):
    if mixer:
        x_ref, at_ref, pool_ref, wa_ref, wp_ref, g_ref, wg_ref, wu_ref, wd_ref, o_ref, a_ref, nat_ref = refs
        sub = FFN_TM // ATT_RES
        for j in range(D_ATTN // LANES):
            for r in range(ATT_RES):
                nat_ref[j, pl.ds(r, sub, stride=ATT_RES), :] = at_ref[r, :, j * LANES:(j + 1) * LANES]
        attn = jnp.concatenate([nat_ref[j] for j in range(D_ATTN // LANES)], axis=1)
        x = (x_ref[...]
             + jnp.dot(attn.astype(BF16), wa_ref[...], preferred_element_type=F32)
             + jnp.dot(pool_ref[...].astype(BF16), wp_ref[...], preferred_element_type=F32))
    else:
        x_ref, g_ref, wg_ref, wu_ref, wd_ref, o_ref, a_ref = refs
        x = x_ref[...]
    h = _rms(x, g_ref[...]).astype(BF16)
    for c in range(D_FF // MXU_COLS):
        sl = slice(c * MXU_COLS, (c + 1) * MXU_COLS)
        gate = jnp.dot(h, wg_ref[:, sl], preferred_element_type=F32)
        up = jnp.dot(h, wu_ref[:, sl], preferred_element_type=F32)
        a_ref[:, sl] = (gate * jax.nn.sigmoid(gate) * up).astype(BF16)
    o_ref[...] = x + 0.5 * jnp.dot(a_ref[...], wd_ref[...], preferred_element_type=F32)


def _ffn(x2d, g, wg, wu, wd, layer, half, mixer=None):
    t = x2d.shape[0]
    pick = lambda i: (layer, half, 0, 0)
    single = pl.Buffered(1)
    row = lambda i: (i, 0)
    in_specs = [pl.BlockSpec((FFN_TM, D_MODEL), row)]
    args = [x2d]
    scratch = [pltpu.VMEM((FFN_TM, D_FF), BF16)]
    if mixer is not None:
        attn, pool, w_out, j = mixer
        per_seq = attn.shape[2] * ATT_RES // FFN_TM
        in_specs += [
            pl.BlockSpec((None, ATT_RES, FFN_TM // ATT_RES, D_ATTN),
                         lambda i: (i // per_seq, 0, i % per_seq, 0)),
            pl.BlockSpec((FFN_TM, D_POOL), row),
            pl.BlockSpec((None, D_ATTN, D_MODEL), lambda i: (j, 0, 0), pipeline_mode=single),
            pl.BlockSpec((None, D_POOL, D_MODEL), lambda i: (j, 1, 0), pipeline_mode=single),
        ]
        args += [attn, pool, w_out, w_out]
        scratch.append(pltpu.VMEM((D_ATTN // LANES, FFN_TM, LANES), F32))
    in_specs += [
        _const_spec((1, D_MODEL)),
        pl.BlockSpec((None, None, D_MODEL, D_FF), pick, pipeline_mode=single),
        pl.BlockSpec((None, None, D_MODEL, D_FF), pick, pipeline_mode=single),
        pl.BlockSpec((None, None, D_FF, D_MODEL), pick, pipeline_mode=single),
    ]
    args += [g, wg, wu, wd]
    return pl.pallas_call(
        functools.partial(_ffn_body, mixer=mixer is not None),
        grid=(t // FFN_TM,),
        in_specs=in_specs,
        out_specs=pl.BlockSpec((FFN_TM, D_MODEL), row),
        out_shape=jax.ShapeDtypeStruct((t, D_MODEL), F32),
        scratch_shapes=scratch,
        compiler_params=pltpu.CompilerParams(
            dimension_semantics=("arbitrary",), vmem_limit_bytes=VMEM_LIMIT),
        name="ffn_mix" if mixer is not None else "ffn",
    )(*args)


EV_TM = 1024
POOL_HALO = 16


def _head_norm_rot(t, gn, ones_bd, c128, s128):
    ssq = jnp.dot((t * t).astype(BF16), ones_bd, preferred_element_type=F32)
    y = t * lax.rsqrt(ssq * (1.0 / HEAD_DIM) + EPS) * gn
    lane = lax.broadcasted_iota(jnp.int32, (1, LANES), 1) % HEAD_DIM
    outs = []
    for j in range(D_ATTN // LANES):
        yj = y[:, j * LANES:(j + 1) * LANES]
        swapped = jnp.where(lane < ROT_DIM // 2,
                            pltpu.roll(yj, LANES - ROT_DIM // 2, 1),
                            pltpu.roll(yj, ROT_DIM // 2, 1))
        outs.append(yj * c128 + swapped * s128)
    return outs


def _evproj_body(x_ref, g_ref, w_ref, qn_ref, kn_ref, ones_ref, cs_ref, e_ref,
                 pw_ref, ps_ref, q_ref, k_ref, v_ref, p_ref, carry_ref, stage_ref):
    it = pl.program_id(1)

    @pl.when(it == 0)
    def _():
        carry_ref[...] = jnp.zeros_like(carry_ref)

    x = x_ref[...]
    h = _rms(x, g_ref[...]).astype(BF16)
    proj = jnp.dot(h, w_ref[...], preferred_element_type=F32)
    ones_bd = ones_ref[...]
    cs = cs_ref[...]
    cs_hi = cs.astype(BF16)
    cs_lo = (cs - cs_hi.astype(F32)).astype(BF16)
    spread = (jnp.dot(cs_hi, e_ref[...], preferred_element_type=F32)
              + jnp.dot(cs_lo, e_ref[...], preferred_element_type=F32))
    lane = lax.broadcasted_iota(jnp.int32, (1, LANES), 1) % HEAD_DIM
    c128 = spread[:, :LANES] + jnp.where(lane < ROT_DIM, 0.0, 1.0)
    s128 = spread[:, LANES:]
    qs = _head_norm_rot(proj[:, :D_ATTN], qn_ref[...], ones_bd, c128, s128)
    ks = _head_norm_rot(proj[:, D_ATTN:2 * D_ATTN], kn_ref[...], ones_bd, c128, s128)
    sub = EV_TM // ATT_RES
    for a, (o_ref, slabs) in enumerate((
            (q_ref, qs), (k_ref, ks),
            (v_ref, [proj[:, 2 * D_ATTN + j * LANES:2 * D_ATTN + (j + 1) * LANES]
                     for j in range(D_ATTN // LANES)]))):
        for j in range(D_ATTN // LANES):
            stage_ref[a * (D_ATTN // LANES) + j] = slabs[j]
        for j in range(D_ATTN // LANES):
            for r in range(ATT_RES):
                o_ref[r, :, j * LANES:(j + 1) * LANES] = stage_ref[
                    a * (D_ATTN // LANES) + j, pl.ds(r, sub, stride=ATT_RES), :]

    p = proj[:, 3 * D_ATTN:]
    pe = jnp.concatenate([carry_ref[...], p], axis=0)
    carry_ref[...] = p[EV_TM - POOL_HALO:, :]
    pos1 = (it * EV_TM + 1 + lax.broadcasted_iota(jnp.int32, (EV_TM, 1), 0)).astype(F32)
    for g, w in enumerate(POOL_WINDOWS):
        sl = slice(g * POOL_C, (g + 1) * POOL_C)
        s = pe[:, sl]
        shift = 1
        while shift < w:
            s = s + pltpu.roll(s, shift, 0)
            shift *= 2
        count = jnp.minimum(pos1, float(w))
        pooled = s[POOL_HALO:, :] / count - p[:, sl]
        y = jnp.dot(pooled.astype(BF16), pw_ref[g], preferred_element_type=F32)
        p_ref[:, sl] = y * ps_ref[:, sl]


def _evproj(x, g, w_in, j, qn, kn, ones_bd, cs, spread, pool_w, pool_scale):
    b, s, _ = x.shape
    tok = lambda bi, i: (bi, i, 0)
    out = jax.ShapeDtypeStruct((b, s, D_ATTN), F32)
    res_out = jax.ShapeDtypeStruct((b, ATT_RES, s // ATT_RES, D_ATTN), F32)
    res_spec = pl.BlockSpec((None, ATT_RES, EV_TM // ATT_RES, D_ATTN), lambda bi, i: (bi, 0, i, 0))
    single = pl.Buffered(1)
    return pl.pallas_call(
        _evproj_body,
        grid=(b, s // EV_TM),
        in_specs=[
            pl.BlockSpec((None, EV_TM, D_MODEL), tok),
            _const_spec((1, D_MODEL)),
            pl.BlockSpec((None, D_MODEL, 4 * D_ATTN), lambda bi, i: (j, 0, 0), pipeline_mode=single),
            _const_spec((1, D_ATTN)),
            _const_spec((1, D_ATTN)),
            _const_spec((D_ATTN, D_ATTN)),
            pl.BlockSpec((None, EV_TM, ROT_DIM), tok),
            _const_spec((ROT_DIM, 2 * LANES)),
            pl.BlockSpec((None, len(POOL_WINDOWS), POOL_C, POOL_C), lambda bi, i: (j, 0, 0, 0),
                         pipeline_mode=single),
            _const_spec((1, D_POOL)),
        ],
        out_specs=[res_spec] * 3 + [pl.BlockSpec((None, EV_TM, D_ATTN), tok)],
        out_shape=[res_out] * 3 + [out],
        scratch_shapes=[pltpu.VMEM((POOL_HALO, D_POOL), F32),
                        pltpu.VMEM((3 * D_ATTN // LANES, EV_TM, LANES), F32)],
        compiler_params=pltpu.CompilerParams(
            dimension_semantics=("arbitrary", "arbitrary"), vmem_limit_bytes=VMEM_LIMIT),
        name="evproj",
    )(x, g, w_in, qn, kn, ones_bd, cs, spread, pool_w, pool_scale)


ATT_BLK = N_BACK


ATT_UNROLL = 32
ATT_QSCALE = HEAD_DIM ** -0.5 * math.log2(math.e)


def _attn_body(q_ref, k_ref, v_ref, o_ref, bias_ref, *stats, seq):
    lane = lax.broadcasted_iota(jnp.int32, (1, LANES), 1)
    head0 = lane < HEAD_DIM
    units = seq // ATT_BLK
    sub = seq // ATT_RES
    qc = ATT_BLK // ATT_RES
    kc = 2 * qc

    qi = lax.broadcasted_iota(jnp.int32, (ATT_BLK, 2 * ATT_BLK), 0)
    kj = lax.broadcasted_iota(jnp.int32, (ATT_BLK, 2 * ATT_BLK), 1)
    qpos = (qi % qc) * ATT_RES + qi // qc
    kpos = (kj % kc) * ATT_RES + kj // kc
    for idx, (dq, off) in enumerate(((qi - kj, 0), (qi - kj, ATT_BLK),
                                     (qpos - kpos, 0), (qpos - kpos, ATT_BLK))):
        delta = dq + off
        bias_ref[idx] = jnp.where((delta >= 0) & (delta <= N_BACK), 0.0, NEG_BIG)

    ones = jnp.ones((2 * ATT_BLK, LANES), BF16)

    def attend(q2, k2, v2, bias):
        q2 = q2 * ATT_QSCALE
        qq = jnp.concatenate([jnp.where(head0, q2, 0.0), jnp.where(head0, 0.0, q2)],
                             axis=0).astype(BF16)
        vaug = jnp.concatenate([v2.astype(BF16), ones], axis=1)
        sc = lax.dot_general(qq, k2.astype(BF16), (((1,), (1,)), ((), ())),
                             preferred_element_type=F32)
        sc = sc + jnp.concatenate([bias, bias], axis=0)
        m = jnp.max(sc, axis=-1, keepdims=True)
        p = jnp.exp2(sc - m).astype(BF16)
        res = jnp.dot(p, vaug, preferred_element_type=F32)
        return (jnp.where(head0, res[:ATT_BLK, :LANES], res[ATT_BLK:, :LANES]),
                jnp.where(head0, res[:ATT_BLK, LANES:], res[ATT_BLK:, LANES:]),
                jnp.where(head0, m[:ATT_BLK], m[ATT_BLK:]))

    def unit_consecutive(nb, refs):
        q0 = pl.multiple_of(nb * qc, qc)
        k0 = pl.multiple_of(jnp.maximum(nb - 1, 0) * qc, qc)
        q2 = jnp.concatenate([q_ref[c, pl.ds(q0, qc), :] for c in range(ATT_RES)], axis=0)
        k2 = jnp.concatenate([k_ref[c, pl.ds(k0, kc), :] for c in range(ATT_RES)], axis=0)
        v2 = jnp.concatenate([v_ref[c, pl.ds(k0, kc), :] for c in range(ATT_RES)], axis=0)
        outs = attend(q2, k2, v2, bias_ref[2 + jnp.minimum(nb, 1)])
        for ref, val in zip(refs, outs):
            for c in range(ATT_RES):
                ref[c, pl.ds(q0, qc), :] = val[c * qc:(c + 1) * qc]

    def unit_strided(u, refs, step):
        per_res = units // ATT_RES
        r = u // per_res
        w = u % per_res
        nblk = per_res // step
        rr = w // nblk
        nb = w % nblk
        kb = jnp.maximum(nb - 1, 0)

        def rows(blk, n):
            start = rr + step * ATT_BLK * blk
            return pl.ds(start, n) if step == 1 else pl.ds(start, n, stride=step)

        qrows = rows(nb, ATT_BLK)
        krows = rows(kb, 2 * ATT_BLK)
        outs = attend(q_ref[r, qrows, :], k_ref[r, krows, :], v_ref[r, krows, :],
                      bias_ref[jnp.minimum(nb, 1)])
        for ref, val in zip(refs, outs):
            ref[r, qrows, :] = val

    for pi, d in enumerate(DILATIONS):
        refs = stats[3 * pi:3 * pi + 3]

        def group(it, carry, d=d, refs=refs):
            for c in range(ATT_UNROLL):
                u = it * ATT_UNROLL + c
                if d == 1:
                    unit_consecutive(u, refs)
                else:
                    unit_strided(u, refs, d // ATT_RES)
            return carry

        lax.fori_loop(0, units // ATT_UNROLL, group, 0)

    def finish(i, carry):
        per_res = sub // ATT_BLK
        r = i // per_res
        rs = pl.ds(pl.multiple_of((i % per_res) * ATT_BLK, ATT_BLK), ATT_BLK)
        ms = [stats[3 * pi + 2][r, rs, :] for pi in range(len(DILATIONS))]
        m_tot = functools.reduce(jnp.maximum, ms)
        ws = [jnp.exp2(m - m_tot) for m in ms]
        num = sum(w * stats[3 * pi][r, rs, :] for pi, w in enumerate(ws))
        den = sum(w * stats[3 * pi + 1][r, rs, :] for pi, w in enumerate(ws))
        o_ref[r, rs, :] = num / den
        return carry

    lax.fori_loop(0, units, finish, 0, unroll=4)


def _attn(q, k, v):
    b, res, sub, _ = q.shape
    spec = pl.BlockSpec((None, res, sub, LANES), lambda bi, hp: (bi, 0, 0, hp))
    return pl.pallas_call(
        functools.partial(_attn_body, seq=res * sub),
        grid=(b, D_ATTN // LANES),
        in_specs=[spec] * 3,
        out_specs=spec,
        out_shape=jax.ShapeDtypeStruct(q.shape, F32),
        scratch_shapes=([pltpu.VMEM((4, ATT_BLK, 2 * ATT_BLK), F32)]
                        + [pltpu.VMEM((res, sub, LANES), F32)] * (3 * len(DILATIONS))),
        compiler_params=pltpu.CompilerParams(
            dimension_semantics=("arbitrary", "arbitrary"), vmem_limit_bytes=VMEM_LIMIT),
        name="dilated_attn",
    )(q, k, v)


S5_CH = 2


def _cmul(ar, ai, br, bi):
    return ar * br - ai * bi, ar * bi + ai * br


def _s5prep_body(are_ref, aim_ref, ldt_ref, brt_ref, bit_ref, cre_ref, cim_ref,
                 lamk_re_ref, lamk_im_ref, win_re_ref, win_im_ref, wout_re_ref, wout_im_ref, kin_ref):
    lam_re = jnp.minimum(are_ref[...], -1e-4)
    lam_im = aim_ref[...]
    dt = jnp.exp(ldt_ref[...])
    mag = jnp.exp(lam_re * dt)
    lbr = mag * jnp.cos(lam_im * dt)
    lbi = mag * jnp.sin(lam_im * dt)
    den = lam_re * lam_re + lam_im * lam_im
    nre = lbr - 1.0
    cr = (nre * lam_re + lbi * lam_im) / den
    ci = (lbi * lam_re - nre * lam_im) / den
    bbr, bbi = _cmul(cr[:, None, :], ci[:, None, :], brt_ref[...], bit_ref[...])
    c_re = cre_ref[...]
    c_im = cim_ref[...]

    powers = [(jnp.ones_like(lbr), jnp.zeros_like(lbr))]
    for _ in range(S5_CH):
        powers.append(_cmul(powers[-1][0], powers[-1][1], lbr, lbi))
    lamk_re_ref[...], lamk_im_ref[...] = powers[S5_CH]

    contract_p = (((2,), (2,)), ((0,), (0,)))
    for i in range(S5_CH):
        pr, pi = powers[S5_CH - 1 - i]
        win_re_ref[i], win_im_ref[i] = _cmul(pr[:, None, :], pi[:, None, :], bbr, bbi)
        qr, qi = powers[i + 1]
        er, ei = _cmul(c_re, c_im, qr[:, None, :], qi[:, None, :])
        wout_re_ref[i] = er
        wout_im_ref[i] = -ei
        er, ei = _cmul(c_re, c_im, powers[i][0][:, None, :], powers[i][1][:, None, :])
        kin_ref[i] = (lax.dot_general(bbr, er, contract_p, precision=lax.Precision.HIGHEST,
                                      preferred_element_type=F32)
                      - lax.dot_general(bbi, ei, contract_p, precision=lax.Precision.HIGHEST,
                                        preferred_element_type=F32))


def _s5prep(a_re, a_im, log_dt, b_re_t, b_im_t, c_re, c_im):
    gp = jax.ShapeDtypeStruct((S5_GROUPS, S5_STATE), F32)
    kgcp = jax.ShapeDtypeStruct((S5_CH, S5_GROUPS, S5_GROUP, S5_STATE), F32)
    kgcc = jax.ShapeDtypeStruct((S5_CH, S5_GROUPS, S5_GROUP, S5_GROUP), F32)
    return pl.pallas_call(
        _s5prep_body,
        out_shape=[gp, gp, kgcp, kgcp, kgcp, kgcp, kgcc],
        name="s5prep",
    )(a_re, a_im, log_dt, b_re_t, b_im_t, c_re, c_im)


S5_LC = 64
S5_SLABS = D_MODEL // LANES
S5_HALF = LANES // S5_GROUP * S5_STATE


def _gelu_tanh(y):
    return 0.5 * y * (1.0 + jnp.tanh(math.sqrt(2.0 / math.pi) * (y + 0.044715 * (y * y * y))))


def _spread_groups(m, rep):
    w = m.shape[-1]
    acc = jnp.zeros((m.shape[0], rep.shape[1]), F32)
    for _ in range(3):
        part = m.astype(BF16)
        m = m - part.astype(F32)
        acc = acc + jnp.dot(part, rep, preferred_element_type=F32)
    row_group = lax.broadcasted_iota(jnp.int32, acc.shape, 0) // S5_GROUP
    col_group = lax.broadcasted_iota(jnp.int32, acc.shape, 1) // w
    return jnp.where(row_group == col_group, acc, 0.0)


def _s5_build_weights(cwin_ref, cwout_ref, ckin_ref, rep_p_ref, rep_c_ref, win_ref, wout_ref, tin_ref):
    rep_p = rep_p_ref[...]
    rep_c = rep_c_ref[...]

    def build(j, carry):
        for i in range(S5_CH):
            rows = slice(i * LANES, (i + 1) * LANES)
            blk = jnp.concatenate([_spread_groups(cwin_ref[0, i, j], rep_p),
                                   _spread_groups(cwin_ref[1, i, j], rep_p)], axis=1)
            win_ref[j, rows, :] = blk.astype(BF16)
            blk = jnp.concatenate([_spread_groups(cwout_ref[0, i, j], rep_p),
                                   _spread_groups(cwout_ref[1, i, j], rep_p)], axis=1)
            wout_ref[j, :, rows] = blk.T.astype(BF16)
        lag = [_spread_groups(ckin_ref[dl, j], rep_c).astype(BF16) for dl in range(S5_CH)]
        zero = jnp.zeros((LANES, LANES), BF16)
        for a in range(S5_CH):
            for b in range(S5_CH):
                tin_ref[j, a * LANES:(a + 1) * LANES, b * LANES:(b + 1) * LANES] = lag[b - a] if b >= a else zero
        return carry

    lax.fori_loop(0, S5_SLABS, build, 0)


def _s5_body(x_ref, g_ref, cwin_ref, cwout_ref, ckin_ref, rep_p_ref, rep_c_ref, lamk_ref, d_ref, wglu_ref,
             o_ref, win_ref, wout_ref, tin_ref, utb_ref, st_ref, carry_ref, z_ref, gtb_ref, *, batch):
    nch = S5_LC // S5_CH
    crows = nch * batch
    rows = S5_LC * batch

    @pl.when(pl.program_id(0) == 0)
    def _():
        carry_ref[...] = jnp.zeros_like(carry_ref)
        _s5_build_weights(cwin_ref, cwout_ref, ckin_ref, rep_p_ref, rep_c_ref, win_ref, wout_ref, tin_ref)

    for b in range(batch):
        hb = _rms(x_ref[b], g_ref[...])
        for j in range(S5_SLABS):
            utb_ref[j, pl.ds(b, S5_LC, stride=batch), :] = hb[:, j * LANES:(j + 1) * LANES]

    for j in range(S5_SLABS):
        sl = slice(j * LANES, (j + 1) * LANES)
        u = utb_ref[j]
        u4 = u.reshape(nch, S5_CH, batch, LANES)
        ucat = jnp.concatenate([u4[:, i].reshape(crows, LANES) for i in range(S5_CH)],
                               axis=1).astype(BF16)
        st_ref[j] = jnp.dot(ucat, win_ref[j], preferred_element_type=F32)

        ar = lamk_ref[j, :, :S5_HALF]
        ai = lamk_ref[j, :, S5_HALF:]
        hr = carry_ref[j, :, :S5_HALF]
        hi = carry_ref[j, :, S5_HALF:]
        for c in range(nch):
            rs = pl.ds(c * batch, batch)
            gr = st_ref[j, rs, :S5_HALF]
            gi = st_ref[j, rs, S5_HALF:]
            st_ref[j, rs, :S5_HALF] = hr
            st_ref[j, rs, S5_HALF:] = hi
            hr, hi = ar * hr - ai * hi + gr, ar * hi + ai * hr + gi
        carry_ref[j, :, :S5_HALF] = hr
        carry_ref[j, :, S5_HALF:] = hi

        y4 = (jnp.dot(st_ref[j].astype(BF16), wout_ref[j], preferred_element_type=F32)
              + jnp.dot(ucat, tin_ref[j], preferred_element_type=F32))
        y = jnp.concatenate([y4[:, i * LANES:(i + 1) * LANES].reshape(nch, 1, batch, LANES)
                             for i in range(S5_CH)], axis=1).reshape(rows, LANES)
        y = y + d_ref[:, sl] * u
        z_ref[:, sl] = _gelu_tanh(y).astype(BF16)

    z = z_ref[...]
    for c in range(D_MODEL // MXU_COLS):
        val = jnp.dot(z, wglu_ref[:, c * MXU_COLS:(c + 1) * MXU_COLS], preferred_element_type=F32)
        gate = jnp.dot(z, wglu_ref[:, D_MODEL + c * MXU_COLS:D_MODEL + (c + 1) * MXU_COLS],
                       preferred_element_type=F32)
        glu = val * jax.nn.sigmoid(gate)
        for h in range(MXU_COLS // LANES):
            gtb_ref[c * (MXU_COLS // LANES) + h] = glu[:, h * LANES:(h + 1) * LANES]

    for b in range(batch):
        for j in range(S5_SLABS):
            sl = slice(j * LANES, (j + 1) * LANES)
            o_ref[b, :, sl] = x_ref[b, :, sl] + gtb_ref[j, pl.ds(b, S5_LC, stride=batch), :]


def _s5(x, g, cwin, cwout, ckin, rep_p, rep_c, lamk, d_skip, w_glu):
    b, s, _ = x.shape
    assert b == SUBLANES
    rows = S5_LC * b
    crows = rows // S5_CH
    blk = pl.BlockSpec((b, S5_LC, D_MODEL), lambda i: (0, i, 0))
    return pl.pallas_call(
        functools.partial(_s5_body, batch=b),
        grid=(s // S5_LC,),
        in_specs=[
            blk,
            _const_spec((1, D_MODEL)),
            _const_spec(cwin.shape),
            _const_spec(cwout.shape),
            _const_spec(ckin.shape),
            _const_spec(rep_p.shape),
            _const_spec(rep_c.shape),
            _const_spec((S5_SLABS, SUBLANES, 2 * S5_HALF)),
            _const_spec((1, D_MODEL)),
            _const_spec((D_MODEL, 2 * D_MODEL)),
        ],
        out_specs=blk,
        out_shape=jax.ShapeDtypeStruct(x.shape, F32),
        scratch_shapes=[
            pltpu.VMEM((S5_SLABS, S5_CH * LANES, 2 * S5_HALF), BF16),
            pltpu.VMEM((S5_SLABS, 2 * S5_HALF, S5_CH * LANES), BF16),
            pltpu.VMEM((S5_SLABS, S5_CH * LANES, S5_CH * LANES), BF16),
            pltpu.VMEM((S5_SLABS, rows, LANES), F32),
            pltpu.VMEM((S5_SLABS, crows, 2 * S5_HALF), F32),
            pltpu.VMEM((S5_SLABS, SUBLANES, 2 * S5_HALF), F32),
            pltpu.VMEM((rows, D_MODEL), BF16),
            pltpu.VMEM((S5_SLABS, rows, LANES), F32),
        ],
        compiler_params=pltpu.CompilerParams(
            dimension_semantics=("arbitrary",), vmem_limit_bytes=VMEM_LIMIT),
        name="s5_glu",
    )(x, g, cwin, cwout, ckin, rep_p, rep_c, lamk, d_skip, w_glu)


def _rotary_tables(positions):
    half = ROT_DIM // 2
    inv_freq = ROPE_THETA ** (-jnp.arange(0, ROT_DIM, 2, dtype=F32) / ROT_DIM)
    ang = inv_freq[:, None, None] * positions.astype(F32)[None]
    cs = jnp.concatenate([jnp.cos(ang), jnp.sin(ang)], axis=0).transpose(1, 2, 0)
    lane = jnp.arange(LANES) % HEAD_DIM
    freq = jnp.arange(half)[:, None]
    first = (lane[None, :] == freq).astype(F32)
    second = (lane[None, :] == freq + half).astype(F32)
    zeros = jnp.zeros_like(first)
    spread = jnp.concatenate([
        jnp.concatenate([first + second, zeros], axis=1),
        jnp.concatenate([zeros, second - first], axis=1),
    ], axis=0).astype(BF16)
    return cs, spread


def _s5_operands(lamk_re, lamk_im, win_re, win_im, wout_re, wout_im, kin_t):
    gl = LANES // S5_GROUP
    by_slab = lambda m: m.reshape(m.shape[:-3] + (S5_SLABS, LANES, m.shape[-1]))
    cwin = by_slab(jnp.stack([win_re, win_im]))
    cwout = by_slab(jnp.stack([wout_re, wout_im]))
    ckin = by_slab(kin_t)
    rep_p = jnp.tile(jnp.eye(S5_STATE, dtype=BF16), (1, gl))
    rep_c = jnp.tile(jnp.eye(S5_GROUP, dtype=BF16), (1, gl))
    lamk = jnp.concatenate([lamk_re.reshape(S5_SLABS, 1, S5_HALF), lamk_im.reshape(S5_SLABS, 1, S5_HALF)],
                           axis=-1)
    lamk = jnp.broadcast_to(lamk, (S5_SLABS, SUBLANES, 2 * S5_HALF))
    return cwin, cwout, ckin, rep_p, rep_c, lamk


def kernel(x, positions, ffn_norm, ffn_w_gate, ffn_w_up, ffn_w_down, mix_norm, ev_w_in, ev_q_norm, ev_k_norm, ev_pool_w, ev_pool_scale, ev_w_out, s5_a_re, s5_a_im, s5_log_dt, s5_b_re, s5_b_im, s5_c_re, s5_c_im, s5_d, s5_w_glu):
    b, s, d = x.shape
    t = b * s
    depth = ffn_norm.shape[0]
    cs, spread = _rotary_tables(positions)
    ones_bd = jnp.kron(jnp.eye(D_ATTN // HEAD_DIM, dtype=F32),
                       jnp.ones((HEAD_DIM, HEAD_DIM), F32)).astype(BF16)
    wg, wu, wd = ffn_w_gate.astype(BF16), ffn_w_up.astype(BF16), ffn_w_down.astype(BF16)
    w_in, pool_w, w_out = ev_w_in.astype(BF16), ev_pool_w.astype(BF16), ev_w_out.astype(BF16)

    def ffn(xx, layer, half, mixer=None):
        return _ffn(xx.reshape(t, d), ffn_norm[layer, half][None, :],
                    wg, wu, wd, layer, half, mixer).reshape(b, s, d)

    for layer in range(depth):
        x = ffn(x, layer, 0)
        j = layer // 2
        g = mix_norm[layer][None, :]
        mixer = None
        if layer % 2 == 0:
            q, k, v, pool = _evproj(
                x, g, w_in, j,
                jnp.tile(ev_q_norm[j], D_ATTN // HEAD_DIM)[None, :],
                jnp.tile(ev_k_norm[j], D_ATTN // HEAD_DIM)[None, :],
                ones_bd, cs, spread, pool_w, ev_pool_scale[j][None, :])
            mixer = (_attn(q, k, v), pool.reshape(t, D_POOL), w_out, j)
        else:
            prep = _s5prep(
                s5_a_re[j], s5_a_im[j], s5_log_dt[j][:, None],
                s5_b_re[j].transpose(0, 2, 1), s5_b_im[j].transpose(0, 2, 1), s5_c_re[j], s5_c_im[j])
            x = _s5(x, g, *_s5_operands(*prep), s5_d[j][None, :], s5_w_glu[j].astype(BF16))
        x = ffn(x, layer, 1, mixer)
    return x
```

```python
import functools
import math

import jax
import jax.numpy as jnp
from jax import lax
from jax.experimental import pallas as pl
from jax.experimental.pallas import tpu as pltpu

F32 = jnp.float32
BF16 = jnp.bfloat16

D_MODEL = 1024
D_ATTN = 512
HEAD_DIM = 64
ROT_DIM = 16
ROPE_THETA = 500000.0
DILATIONS = (1, 4, 16)
N_BACK = 128
D_POOL = 512
POOL_WINDOWS = (2, 4, 8, 16)
POOL_C = 128
S5_GROUP = 16
S5_GROUPS = 64
S5_STATE = 64
D_FF = 2816
EPS = 1e-6

LANES = 128
SUBLANES = 8
MXU_COLS = 256
VMEM_LIMIT = 56 * 1024 * 1024

NEG_BIG = -1e30


def _rms(x, g):
    return x * lax.rsqrt(jnp.mean(x * x, axis=-1, keepdims=True) + EPS) * g


def _const_spec(shape):
    nd = len(shape)
    return pl.BlockSpec(shape, lambda *_: (0,) * nd, pipeline_mode=pl.Buffered(1))


FFN_TM = 1024
ATT_RES = 4


def _ffn_body(*refs, mixer):
    if mixer:
        x_ref, at_ref, pool_ref, wa_ref, wp_ref, g_ref, wg_ref, wu_ref, wd_ref, o_ref, a_ref = refs
        x = (x_ref[...]
             + jnp.dot(at_ref[...].astype(BF16), wa_ref[...], preferred_element_type=F32)
             + jnp.dot(pool_ref[...].astype(BF16), wp_ref[...], preferred_element_type=F32))
    else:
        x_ref, g_ref, wg_ref, wu_ref, wd_ref, o_ref, a_ref = refs
        x = x_ref[...]
    h = _rms(x, g_ref[...]).astype(BF16)
    for c in range(D_FF // MXU_COLS):
        sl = slice(c * MXU_COLS, (c + 1) * MXU_COLS)
        gate = jnp.dot(h, wg_ref[:, sl], preferred_element_type=F32)
        up = jnp.dot(h, wu_ref[:, sl], preferred_element_type=F32)
        a_ref[:, sl] = (gate * jax.nn.sigmoid(gate) * up).astype(BF16)
    o_ref[...] = x + 0.5 * jnp.dot(a_ref[...], wd_ref[...], preferred_element_type=F32)


def _ffn(x2d, g, wg, wu, wd, layer, half, mixer=None):
    t = x2d.shape[0]
    pick = lambda i: (layer, half, 0, 0)
    single = pl.Buffered(1)
    row = lambda i: (i, 0)
    in_specs = [pl.BlockSpec((FFN_TM, D_MODEL), row)]
    args = [x2d]
    scratch = [pltpu.VMEM((FFN_TM, D_FF), BF16)]
    if mixer is not None:
        attn, pool, w_out, j = mixer
        in_specs += [
            pl.BlockSpec((FFN_TM, D_ATTN), row),
            pl.BlockSpec((FFN_TM, D_POOL), row),
            pl.BlockSpec((None, D_ATTN, D_MODEL), lambda i: (j, 0, 0), pipeline_mode=single),
            pl.BlockSpec((None, D_POOL, D_MODEL), lambda i: (j, 1, 0), pipeline_mode=single),
        ]
        args += [attn, pool, w_out, w_out]
    in_specs += [
        _const_spec((1, D_MODEL)),
        pl.BlockSpec((None, None, D_MODEL, D_FF), pick, pipeline_mode=single),
        pl.BlockSpec((None, None, D_MODEL, D_FF), pick, pipeline_mode=single),
        pl.BlockSpec((None, None, D_FF, D_MODEL), pick, pipeline_mode=single),
    ]
    args += [g, wg, wu, wd]
    return pl.pallas_call(
        functools.partial(_ffn_body, mixer=mixer is not None),
        grid=(t // FFN_TM,),
        in_specs=in_specs,
        out_specs=pl.BlockSpec((FFN_TM, D_MODEL), row),
        out_shape=jax.ShapeDtypeStruct((t, D_MODEL), F32),
        scratch_shapes=scratch,
        compiler_params=pltpu.CompilerParams(
            dimension_semantics=("arbitrary",), vmem_limit_bytes=VMEM_LIMIT),
        name="ffn_mix" if mixer is not None else "ffn",
    )(*args)


EV_TM = 1024
POOL_HALO = 16


def _head_norm_rot(t, gn, ones_bd, c128, s128):
    ssq = jnp.dot((t * t).astype(BF16), ones_bd, preferred_element_type=F32)
    y = t * lax.rsqrt(ssq * (1.0 / HEAD_DIM) + EPS) * gn
    lane = lax.broadcasted_iota(jnp.int32, (1, LANES), 1) % HEAD_DIM
    outs = []
    for j in range(D_ATTN // LANES):
        yj = y[:, j * LANES:(j + 1) * LANES]
        swapped = jnp.where(lane < ROT_DIM // 2,
                            pltpu.roll(yj, LANES - ROT_DIM // 2, 1),
                            pltpu.roll(yj, ROT_DIM // 2, 1))
        outs.append(yj * c128 + swapped * s128)
    return outs


def _evproj_body(x_ref, g_ref, w_ref, qn_ref, kn_ref, ones_ref, cs_ref, e_ref,
                 pw_ref, ps_ref, q_ref, k_ref, v_ref, p_ref, carry_ref, stage_ref):
    it = pl.program_id(1)

    @pl.when(it == 0)
    def _():
        carry_ref[...] = jnp.zeros_like(carry_ref)

    x = x_ref[...]
    h = _rms(x, g_ref[...]).astype(BF16)
    proj = jnp.dot(h, w_ref[...], preferred_element_type=F32)
    ones_bd = ones_ref[...]
    cs = cs_ref[...]
    cs_hi = cs.astype(BF16)
    cs_lo = (cs - cs_hi.astype(F32)).astype(BF16)
    spread = (jnp.dot(cs_hi, e_ref[...], preferred_element_type=F32)
              + jnp.dot(cs_lo, e_ref[...], preferred_element_type=F32))
    lane = lax.broadcasted_iota(jnp.int32, (1, LANES), 1) % HEAD_DIM
    c128 = spread[:, :LANES] + jnp.where(lane < ROT_DIM, 0.0, 1.0)
    s128 = spread[:, LANES:]
    qs = _head_norm_rot(proj[:, :D_ATTN], qn_ref[...], ones_bd, c128, s128)
    ks = _head_norm_rot(proj[:, D_ATTN:2 * D_ATTN], kn_ref[...], ones_bd, c128, s128)
    sub = EV_TM // ATT_RES
    for a, (o_ref, slabs) in enumerate((
            (q_ref, qs), (k_ref, ks),
            (v_ref, [proj[:, 2 * D_ATTN + j * LANES:2 * D_ATTN + (j + 1) * LANES]
                     for j in range(D_ATTN // LANES)]))):
        for j in range(D_ATTN // LANES):
            stage_ref[a * (D_ATTN // LANES) + j] = slabs[j]
        for j in range(D_ATTN // LANES):
            for r in range(ATT_RES):
                o_ref[r, :, j * LANES:(j + 1) * LANES] = stage_ref[
                    a * (D_ATTN // LANES) + j, pl.ds(r, sub, stride=ATT_RES), :]

    p = proj[:, 3 * D_ATTN:]
    pe = jnp.concatenate([carry_ref[...], p], axis=0)
    carry_ref[...] = p[EV_TM - POOL_HALO:, :]
    pos1 = (it * EV_TM + 1 + lax.broadcasted_iota(jnp.int32, (EV_TM, 1), 0)).astype(F32)
    for g, w in enumerate(POOL_WINDOWS):
        sl = slice(g * POOL_C, (g + 1) * POOL_C)
        s = pe[:, sl]
        shift = 1
        while shift < w:
            s = s + pltpu.roll(s, shift, 0)
            shift *= 2
        count = jnp.minimum(pos1, float(w))
        pooled = s[POOL_HALO:, :] / count - p[:, sl]
        y = jnp.dot(pooled.astype(BF16), pw_ref[g], preferred_element_type=F32)
        p_ref[:, sl] = y * ps_ref[:, sl]


def _evproj(x, g, w_in, j, qn, kn, ones_bd, cs, spread, pool_w, pool_scale):
    b, s, _ = x.shape
    tok = lambda bi, i: (bi, i, 0)
    out = jax.ShapeDtypeStruct((b, s, D_ATTN), F32)
    res_out = jax.ShapeDtypeStruct((b, ATT_RES, s // ATT_RES, D_ATTN), F32)
    res_spec = pl.BlockSpec((None, ATT_RES, EV_TM // ATT_RES, D_ATTN), lambda bi, i: (bi, 0, i, 0))
    single = pl.Buffered(1)
    return pl.pallas_call(
        _evproj_body,
        grid=(b, s // EV_TM),
        in_specs=[
            pl.BlockSpec((None, EV_TM, D_MODEL), tok),
            _const_spec((1, D_MODEL)),
            pl.BlockSpec((None, D_MODEL, 4 * D_ATTN), lambda bi, i: (j, 0, 0), pipeline_mode=single),
            _const_spec((1, D_ATTN)),
            _const_spec((1, D_ATTN)),
            _const_spec((D_ATTN, D_ATTN)),
            pl.BlockSpec((None, EV_TM, ROT_DIM), tok),
            _const_spec((ROT_DIM, 2 * LANES)),
            pl.BlockSpec((None, len(POOL_WINDOWS), POOL_C, POOL_C), lambda bi, i: (j, 0, 0, 0),
                         pipeline_mode=single),
            _const_spec((1, D_POOL)),
        ],
        out_specs=[res_spec] * 3 + [pl.BlockSpec((None, EV_TM, D_ATTN), tok)],
        out_shape=[res_out] * 3 + [out],
        scratch_shapes=[pltpu.VMEM((POOL_HALO, D_POOL), F32),
                        pltpu.VMEM((3 * D_ATTN // LANES, EV_TM, LANES), F32)],
        compiler_params=pltpu.CompilerParams(
            dimension_semantics=("arbitrary", "arbitrary"), vmem_limit_bytes=VMEM_LIMIT),
        name="evproj",
    )(x, g, w_in, qn, kn, ones_bd, cs, spread, pool_w, pool_scale)


ATT_BLK = N_BACK


ATT_UNROLL = 32
ATT_QSCALE = HEAD_DIM ** -0.5 * math.log2(math.e)


def _attn_body(q_ref, k_ref, v_ref, o_ref, bias_ref, *stats, seq):
    lane = lax.broadcasted_iota(jnp.int32, (1, LANES), 1)
    head0 = lane < HEAD_DIM
    units = seq // ATT_BLK
    sub = seq // ATT_RES
    qc = ATT_BLK // ATT_RES
    kc = 2 * qc

    qi = lax.broadcasted_iota(jnp.int32, (ATT_BLK, 2 * ATT_BLK), 0)
    kj = lax.broadcasted_iota(jnp.int32, (ATT_BLK, 2 * ATT_BLK), 1)
    qpos = (qi % qc) * ATT_RES + qi // qc
    kpos = (kj % kc) * ATT_RES + kj // kc
    for idx, (dq, off) in enumerate(((qi - kj, 0), (qi - kj, ATT_BLK),
                                     (qpos - kpos, 0), (qpos - kpos, ATT_BLK))):
        delta = dq + off
        bias_ref[idx] = jnp.where((delta >= 0) & (delta <= N_BACK), 0.0, NEG_BIG)

    ones = jnp.ones((2 * ATT_BLK, LANES), BF16)

    def attend(q2, k2, v2, bias):
        q2 = q2 * ATT_QSCALE
        qq = jnp.concatenate([jnp.where(head0, q2, 0.0), jnp.where(head0, 0.0, q2)],
                             axis=0).astype(BF16)
        vaug = jnp.concatenate([v2.astype(BF16), ones], axis=1)
        sc = lax.dot_general(qq, k2.astype(BF16), (((1,), (1,)), ((), ())),
                             preferred_element_type=F32)
        sc = sc + jnp.concatenate([bias, bias], axis=0)
        m = jnp.max(sc, axis=-1, keepdims=True)
        p = jnp.exp2(sc - m).astype(BF16)
        res = jnp.dot(p, vaug, preferred_element_type=F32)
        return (jnp.where(head0, res[:ATT_BLK, :LANES], res[ATT_BLK:, :LANES]),
                jnp.where(head0, res[:ATT_BLK, LANES:], res[ATT_BLK:, LANES:]),
                jnp.where(head0, m[:ATT_BLK], m[ATT_BLK:]))

    def unit_consecutive(nb, refs):
        q0 = pl.multiple_of(nb * qc, qc)
        k0 = pl.multiple_of(jnp.maximum(nb - 1, 0) * qc, qc)
        q2 = jnp.concatenate([q_ref[c, pl.ds(q0, qc), :] for c in range(ATT_RES)], axis=0)
        k2 = jnp.concatenate([k_ref[c, pl.ds(k0, kc), :] for c in range(ATT_RES)], axis=0)
        v2 = jnp.concatenate([v_ref[c, pl.ds(k0, kc), :] for c in range(ATT_RES)], axis=0)
        outs = attend(q2, k2, v2, bias_ref[2 + jnp.minimum(nb, 1)])
        for ref, val in zip(refs, outs):
            for c in range(ATT_RES):
                ref[c, pl.ds(q0, qc), :] = val[c * qc:(c + 1) * qc]

    def unit_strided(u, refs, step):
        per_res = units // ATT_RES
        r = u // per_res
        w = u % per_res
        nblk = per_res // step
        rr = w // nblk
        nb = w % nblk
        kb = jnp.maximum(nb - 1, 0)

        def rows(blk, n):
            start = rr + step * ATT_BLK * blk
            return pl.ds(start, n) if step == 1 else pl.ds(start, n, stride=step)

        qrows = rows(nb, ATT_BLK)
        krows = rows(kb, 2 * ATT_BLK)
        outs = attend(q_ref[r, qrows, :], k_ref[r, krows, :], v_ref[r, krows, :],
                      bias_ref[jnp.minimum(nb, 1)])
        for ref, val in zip(refs, outs):
            ref[r, qrows, :] = val

    for pi, d in enumerate(DILATIONS):
        refs = stats[3 * pi:3 * pi + 3]

        def group(it, carry, d=d, refs=refs):
            for c in range(ATT_UNROLL):
                u = it * ATT_UNROLL + c
                if d == 1:
                    unit_consecutive(u, refs)
                else:
                    unit_strided(u, refs, d // ATT_RES)
            return carry

        lax.fori_loop(0, units // ATT_UNROLL, group, 0)

    def finish(i, carry):
        per_res = sub // ATT_BLK
        r = i // per_res
        row0 = pl.multiple_of((i % per_res) * ATT_BLK, ATT_BLK)
        rs = pl.ds(row0, ATT_BLK)
        ms = [stats[3 * pi + 2][r, rs, :] for pi in range(len(DILATIONS))]
        m_tot = functools.reduce(jnp.maximum, ms)
        ws = [jnp.exp2(m - m_tot) for m in ms]
        num = sum(w * stats[3 * pi][r, rs, :] for pi, w in enumerate(ws))
        den = sum(w * stats[3 * pi + 1][r, rs, :] for pi, w in enumerate(ws))
        o_ref[pl.ds(r + ATT_RES * row0, ATT_BLK, stride=ATT_RES), :] = num / den
        return carry

    lax.fori_loop(0, units, finish, 0, unroll=4)


def _attn(q, k, v):
    b, res, sub, _ = q.shape
    spec = pl.BlockSpec((None, res, sub, LANES), lambda bi, hp: (bi, 0, 0, hp))
    return pl.pallas_call(
        functools.partial(_attn_body, seq=res * sub),
        grid=(b, D_ATTN // LANES),
        in_specs=[spec] * 3,
        out_specs=pl.BlockSpec((None, res * sub, LANES), lambda bi, hp: (bi, 0, hp)),
        out_shape=jax.ShapeDtypeStruct((b, res * sub, D_ATTN), F32),
        scratch_shapes=([pltpu.VMEM((4, ATT_BLK, 2 * ATT_BLK), F32)]
                        + [pltpu.VMEM((res, sub, LANES), F32)] * (3 * len(DILATIONS))),
        compiler_params=pltpu.CompilerParams(
            dimension_semantics=("arbitrary", "arbitrary"), vmem_limit_bytes=VMEM_LIMIT),
        name="dilated_attn",
    )(q, k, v)


S5_CH = 2


def _cmul(ar, ai, br, bi):
    return ar * br - ai * bi, ar * bi + ai * br


def _s5prep_body(are_ref, aim_ref, ldt_ref, brt_ref, bit_ref, cre_ref, cim_ref,
                 lamk_re_ref, lamk_im_ref, win_re_ref, win_im_ref, wout_re_ref, wout_im_ref, kin_ref):
    lam_re = jnp.minimum(are_ref[...], -1e-4)
    lam_im = aim_ref[...]
    dt = jnp.exp(ldt_ref[...])
    mag = jnp.exp(lam_re * dt)
    lbr = mag * jnp.cos(lam_im * dt)
    lbi = mag * jnp.sin(lam_im * dt)
    den = lam_re * lam_re + lam_im * lam_im
    nre = lbr - 1.0
    cr = (nre * lam_re + lbi * lam_im) / den
    ci = (lbi * lam_re - nre * lam_im) / den
    bbr, bbi = _cmul(cr[:, None, :], ci[:, None, :], brt_ref[...], bit_ref[...])
    c_re = cre_ref[...]
    c_im = cim_ref[...]

    powers = [(jnp.ones_like(lbr), jnp.zeros_like(lbr))]
    for _ in range(S5_CH):
        powers.append(_cmul(powers[-1][0], powers[-1][1], lbr, lbi))
    lamk_re_ref[...], lamk_im_ref[...] = powers[S5_CH]

    contract_p = (((2,), (2,)), ((0,), (0,)))
    for i in range(S5_CH):
        pr, pi = powers[S5_CH - 1 - i]
        win_re_ref[i], win_im_ref[i] = _cmul(pr[:, None, :], pi[:, None, :], bbr, bbi)
        qr, qi = powers[i + 1]
        er, ei = _cmul(c_re, c_im, qr[:, None, :], qi[:, None, :])
        wout_re_ref[i] = er
        wout_im_ref[i] = -ei
        er, ei = _cmul(c_re, c_im, powers[i][0][:, None, :], powers[i][1][:, None, :])
        kin_ref[i] = (lax.dot_general(bbr, er, contract_p, precision=lax.Precision.HIGHEST,
                                      preferred_element_type=F32)
                      - lax.dot_general(bbi, ei, contract_p, precision=lax.Precision.HIGHEST,
                                        preferred_element_type=F32))


def _s5prep(a_re, a_im, log_dt, b_re_t, b_im_t, c_re, c_im):
    gp = jax.ShapeDtypeStruct((S5_GROUPS, S5_STATE), F32)
    kgcp = jax.ShapeDtypeStruct((S5_CH, S5_GROUPS, S5_GROUP, S5_STATE), F32)
    kgcc = jax.ShapeDtypeStruct((S5_CH, S5_GROUPS, S5_GROUP, S5_GROUP), F32)
    return pl.pallas_call(
        _s5prep_body,
        out_shape=[gp, gp, kgcp, kgcp, kgcp, kgcp, kgcc],
        name="s5prep",
    )(a_re, a_im, log_dt, b_re_t, b_im_t, c_re, c_im)


S5_LC = 64
S5_SLABS = D_MODEL // LANES
S5_HALF = LANES // S5_GROUP * S5_STATE


def _gelu_tanh(y):
    return 0.5 * y * (1.0 + jnp.tanh(math.sqrt(2.0 / math.pi) * (y + 0.044715 * (y * y * y))))


def _spread_groups(m, rep):
    w = m.shape[-1]
    acc = jnp.zeros((m.shape[0], rep.shape[1]), F32)
    for _ in range(3):
        part = m.astype(BF16)
        m = m - part.astype(F32)
        acc = acc + jnp.dot(part, rep, preferred_element_type=F32)
    row_group = lax.broadcasted_iota(jnp.int32, acc.shape, 0) // S5_GROUP
    col_group = lax.broadcasted_iota(jnp.int32, acc.shape, 1) // w
    return jnp.where(row_group == col_group, acc, 0.0)


def _s5_build_weights(cwin_ref, cwout_ref, ckin_ref, rep_p_ref, rep_c_ref, win_ref, wout_ref, tin_ref):
    rep_p = rep_p_ref[...]
    rep_c = rep_c_ref[...]

    def build(j, carry):
        for i in range(S5_CH):
            rows = slice(i * LANES, (i + 1) * LANES)
            blk = jnp.concatenate([_spread_groups(cwin_ref[0, i, j], rep_p),
                                   _spread_groups(cwin_ref[1, i, j], rep_p)], axis=1)
            win_ref[j, rows, :] = blk.astype(BF16)
            blk = jnp.concatenate([_spread_groups(cwout_ref[0, i, j], rep_p),
                                   _spread_groups(cwout_ref[1, i, j], rep_p)], axis=1)
            wout_ref[j, :, rows] = blk.T.astype(BF16)
        lag = [_spread_groups(ckin_ref[dl, j], rep_c).astype(BF16) for dl in range(S5_CH)]
        zero = jnp.zeros((LANES, LANES), BF16)
        for a in range(S5_CH):
            for b in range(S5_CH):
                tin_ref[j, a * LANES:(a + 1) * LANES, b * LANES:(b + 1) * LANES] = lag[b - a] if b >= a else zero
        return carry

    lax.fori_loop(0, S5_SLABS, build, 0)


def _s5_body(x_ref, g_ref, cwin_ref, cwout_ref, ckin_ref, rep_p_ref, rep_c_ref, lamk_ref, d_ref, wglu_ref,
             o_ref, win_ref, wout_ref, tin_ref, utb_ref, st_ref, carry_ref, z_ref, gtb_ref, *, batch):
    nch = S5_LC // S5_CH
    crows = nch * batch
    rows = S5_LC * batch

    @pl.when(pl.program_id(0) == 0)
    def _():
        carry_ref[...] = jnp.zeros_like(carry_ref)
        _s5_build_weights(cwin_ref, cwout_ref, ckin_ref, rep_p_ref, rep_c_ref, win_ref, wout_ref, tin_ref)

    for b in range(batch):
        hb = _rms(x_ref[b], g_ref[...])
        for j in range(S5_SLABS):
            utb_ref[j, pl.ds(b, S5_LC, stride=batch), :] = hb[:, j * LANES:(j + 1) * LANES]

    for j in range(S5_SLABS):
        sl = slice(j * LANES, (j + 1) * LANES)
        u = utb_ref[j]
        u4 = u.reshape(nch, S5_CH, batch, LANES)
        ucat = jnp.concatenate([u4[:, i].reshape(crows, LANES) for i in range(S5_CH)],
                               axis=1).astype(BF16)
        st_ref[j] = jnp.dot(ucat, win_ref[j], preferred_element_type=F32)

        ar = lamk_ref[j, :, :S5_HALF]
        ai = lamk_ref[j, :, S5_HALF:]
        hr = carry_ref[j, :, :S5_HALF]
        hi = carry_ref[j, :, S5_HALF:]
        for c in range(nch):
            rs = pl.ds(c * batch, batch)
            gr = st_ref[j, rs, :S5_HALF]
            gi = st_ref[j, rs, S5_HALF:]
            st_ref[j, rs, :S5_HALF] = hr
            st_ref[j, rs, S5_HALF:] = hi
            hr, hi = ar * hr - ai * hi + gr, ar * hi + ai * hr + gi
        carry_ref[j, :, :S5_HALF] = hr
        carry_ref[j, :, S5_HALF:] = hi

        y4 = (jnp.dot(st_ref[j].astype(BF16), wout_ref[j], preferred_element_type=F32)
              + jnp.dot(ucat, tin_ref[j], preferred_element_type=F32))
        y = jnp.concatenate([y4[:, i * LANES:(i + 1) * LANES].reshape(nch, 1, batch, LANES)
                             for i in range(S5_CH)], axis=1).reshape(rows, LANES)
        y = y + d_ref[:, sl] * u
        z_ref[:, sl] = _gelu_tanh(y).astype(BF16)

    z = z_ref[...]
    for c in range(D_MODEL // MXU_COLS):
        val = jnp.dot(z, wglu_ref[:, c * MXU_COLS:(c + 1) * MXU_COLS], preferred_element_type=F32)
        gate = jnp.dot(z, wglu_ref[:, D_MODEL + c * MXU_COLS:D_MODEL + (c + 1) * MXU_COLS],
                       preferred_element_type=F32)
        glu = val * jax.nn.sigmoid(gate)
        for h in range(MXU_COLS // LANES):
            gtb_ref[c * (MXU_COLS // LANES) + h] = glu[:, h * LANES:(h + 1) * LANES]

    for b in range(batch):
        for j in range(S5_SLABS):
            sl = slice(j * LANES, (j + 1) * LANES)
            o_ref[b, :, sl] = x_ref[b, :, sl] + gtb_ref[j, pl.ds(b, S5_LC, stride=batch), :]


def _s5(x, g, cwin, cwout, ckin, rep_p, rep_c, lamk, d_skip, w_glu):
    b, s, _ = x.shape
    assert b == SUBLANES
    rows = S5_LC * b
    crows = rows // S5_CH
    blk = pl.BlockSpec((b, S5_LC, D_MODEL), lambda i: (0, i, 0))
    return pl.pallas_call(
        functools.partial(_s5_body, batch=b),
        grid=(s // S5_LC,),
        in_specs=[
            blk,
            _const_spec((1, D_MODEL)),
            _const_spec(cwin.shape),
            _const_spec(cwout.shape),
            _const_spec(ckin.shape),
            _const_spec(rep_p.shape),
            _const_spec(rep_c.shape),
            _const_spec((S5_SLABS, SUBLANES, 2 * S5_HALF)),
            _const_spec((1, D_MODEL)),
            _const_spec((D_MODEL, 2 * D_MODEL)),
        ],
        out_specs=blk,
        out_shape=jax.ShapeDtypeStruct(x.shape, F32),
        scratch_shapes=[
            pltpu.VMEM((S5_SLABS, S5_CH * LANES, 2 * S5_HALF), BF16),
            pltpu.VMEM((S5_SLABS, 2 * S5_HALF, S5_CH * LANES), BF16),
            pltpu.VMEM((S5_SLABS, S5_CH * LANES, S5_CH * LANES), BF16),
            pltpu.VMEM((S5_SLABS, rows, LANES), F32),
            pltpu.VMEM((S5_SLABS, crows, 2 * S5_HALF), F32),
            pltpu.VMEM((S5_SLABS, SUBLANES, 2 * S5_HALF), F32),
            pltpu.VMEM((rows, D_MODEL), BF16),
            pltpu.VMEM((S5_SLABS, rows, LANES), F32),
        ],
        compiler_params=pltpu.CompilerParams(
            dimension_semantics=("arbitrary",), vmem_limit_bytes=VMEM_LIMIT),
        name="s5_glu",
    )(x, g, cwin, cwout, ckin, rep_p, rep_c, lamk, d_skip, w_glu)


def _rotary_tables(positions):
    half = ROT_DIM // 2
    inv_freq = ROPE_THETA ** (-jnp.arange(0, ROT_DIM, 2, dtype=F32) / ROT_DIM)
    ang = inv_freq[:, None, None] * positions.astype(F32)[None]
    cs = jnp.concatenate([jnp.cos(ang), jnp.sin(ang)], axis=0).transpose(1, 2, 0)
    lane = jnp.arange(LANES) % HEAD_DIM
    freq = jnp.arange(half)[:, None]
    first = (lane[None, :] == freq).astype(F32)
    second = (lane[None, :] == freq + half).astype(F32)
    zeros = jnp.zeros_like(first)
    spread = jnp.concatenate([
        jnp.concatenate([first + second, zeros], axis=1),
        jnp.concatenate([zeros, second - first], axis=1),
    ], axis=0).astype(BF16)
    return cs, spread


def _s5_operands(lamk_re, lamk_im, win_re, win_im, wout_re, wout_im, kin_t):
    gl = LANES // S5_GROUP
    by_slab = lambda m: m.reshape(m.shape[:-3] + (S5_SLABS, LANES, m.shape[-1]))
    cwin = by_slab(jnp.stack([win_re, win_im]))
    cwout = by_slab(jnp.stack([wout_re, wout_im]))
    ckin = by_slab(kin_t)
    rep_p = jnp.tile(jnp.eye(S5_STATE, dtype=BF16), (1, gl))
    rep_c = jnp.tile(jnp.eye(S5_GROUP, dtype=BF16), (1, gl))
    lamk = jnp.concatenate([lamk_re.reshape(S5_SLABS, 1, S5_HALF), lamk_im.reshape(S5_SLABS, 1, S5_HALF)],
                           axis=-1)
    lamk = jnp.broadcast_to(lamk, (S5_SLABS, SUBLANES, 2 * S5_HALF))
    return cwin, cwout, ckin, rep_p, rep_c, lamk


def kernel(x, positions, ffn_norm, ffn_w_gate, ffn_w_up, ffn_w_down, mix_norm, ev_w_in, ev_q_norm, ev_k_norm, ev_pool_w, ev_pool_scale, ev_w_out, s5_a_re, s5_a_im, s5_log_dt, s5_b_re, s5_b_im, s5_c_re, s5_c_im, s5_d, s5_w_glu):
    b, s, d = x.shape
    t = b * s
    depth = ffn_norm.shape[0]
    cs, spread = _rotary_tables(positions)
    ones_bd = jnp.kron(jnp.eye(D_ATTN // HEAD_DIM, dtype=F32),
                       jnp.ones((HEAD_DIM, HEAD_DIM), F32)).astype(BF16)
    wg, wu, wd = ffn_w_gate.astype(BF16), ffn_w_up.astype(BF16), ffn_w_down.astype(BF16)
    w_in, pool_w, w_out = ev_w_in.astype(BF16), ev_pool_w.astype(BF16), ev_w_out.astype(BF16)

    def ffn(xx, layer, half, mixer=None):
        return _ffn(xx.reshape(t, d), ffn_norm[layer, half][None, :],
                    wg, wu, wd, layer, half, mixer).reshape(b, s, d)

    for layer in range(depth):
        x = ffn(x, layer, 0)
        j = layer // 2
        g = mix_norm[layer][None, :]
        mixer = None
        if layer % 2 == 0:
            q, k, v, pool = _evproj(
                x, g, w_in, j,
                jnp.tile(ev_q_norm[j], D_ATTN // HEAD_DIM)[None, :],
                jnp.tile(ev_k_norm[j], D_ATTN // HEAD_DIM)[None, :],
                ones_bd, cs, spread, pool_w, ev_pool_scale[j][None, :])
            mixer = (_attn(q, k, v).reshape(t, D_ATTN), pool.reshape(t, D_POOL), w_out, j)
        else:
            prep = _s5prep(
                s5_a_re[j], s5_a_im[j], s5_log_dt[j][:, None],
                s5_b_re[j].transpose(0, 2, 1), s5_b_im[j].transpose(0, 2, 1), s5_c_re[j], s5_c_im[j])
            x = _s5(x, g, *_s5_operands(*prep), s5_d[j][None, :], s5_w_glu[j].astype(BF16))
        x = ffn(x, layer, 1, mixer)
    return x
```

```python
import functools
import math

import jax
import jax.numpy as jnp
from jax import lax
from jax.experimental import pallas as pl
from jax.experimental.pallas import tpu as pltpu

F32 = jnp.float32
BF16 = jnp.bfloat16

D_MODEL = 1024
D_ATTN = 512
HEAD_DIM = 64
ROT_DIM = 16
ROPE_THETA = 500000.0
DILATIONS = (1, 4, 16)
N_BACK = 128
D_POOL = 512
POOL_WINDOWS = (2, 4, 8, 16)
POOL_C = 128
S5_GROUP = 16
S5_GROUPS = 64
S5_STATE = 64
D_FF = 2816
EPS = 1e-6

LANES = 128
SUBLANES = 8
MXU_COLS = 256
VMEM_LIMIT = 56 * 1024 * 1024

NEG_BIG = -1e30


def _rms(x, g):
    return x * lax.rsqrt(jnp.mean(x * x, axis=-1, keepdims=True) + EPS) * g


def _const_spec(shape):
    nd = len(shape)
    return pl.BlockSpec(shape, lambda *_: (0,) * nd, pipeline_mode=pl.Buffered(1))


FFN_TM = 1024
ATT_RES = 4


def _ffn_body(*refs, mixer):
    if mixer:
        x_ref, at_ref, pool_ref, wa_ref, wp_ref, g_ref, wg_ref, wu_ref, wd_ref, o_ref, a_ref = refs
        x = (x_ref[...]
             + jnp.dot(at_ref[...].astype(BF16), wa_ref[...], preferred_element_type=F32)
             + jnp.dot(pool_ref[...].astype(BF16), wp_ref[...], preferred_element_type=F32))
    else:
        x_ref, g_ref, wg_ref, wu_ref, wd_ref, o_ref, a_ref = refs
        x = x_ref[...]
    h = _rms(x, g_ref[...]).astype(BF16)
    for c in range(D_FF // MXU_COLS):
        sl = slice(c * MXU_COLS, (c + 1) * MXU_COLS)
        gate = jnp.dot(h, wg_ref[:, sl], preferred_element_type=F32)
        up = jnp.dot(h, wu_ref[:, sl], preferred_element_type=F32)
        a_ref[:, sl] = (gate * jax.nn.sigmoid(gate) * up).astype(BF16)
    o_ref[...] = x + 0.5 * jnp.dot(a_ref[...], wd_ref[...], preferred_element_type=F32)


def _ffn(x2d, g, wg, wu, wd, layer, half, mixer=None):
    t = x2d.shape[0]
    pick = lambda i: (layer, half, 0, 0)
    single = pl.Buffered(1)
    row = lambda i: (i, 0)
    in_specs = [pl.BlockSpec((FFN_TM, D_MODEL), row)]
    args = [x2d]
    scratch = [pltpu.VMEM((FFN_TM, D_FF), BF16)]
    if mixer is not None:
        attn, pool, w_out, j = mixer
        in_specs += [
            pl.BlockSpec((FFN_TM, D_ATTN), row),
            pl.BlockSpec((FFN_TM, D_POOL), row),
            pl.BlockSpec((None, D_ATTN, D_MODEL), lambda i: (j, 0, 0), pipeline_mode=single),
            pl.BlockSpec((None, D_POOL, D_MODEL), lambda i: (j, 1, 0), pipeline_mode=single),
        ]
        args += [attn, pool, w_out, w_out]
    in_specs += [
        _const_spec((1, D_MODEL)),
        pl.BlockSpec((None, None, D_MODEL, D_FF), pick, pipeline_mode=single),
        pl.BlockSpec((None, None, D_MODEL, D_FF), pick, pipeline_mode=single),
        pl.BlockSpec((None, None, D_FF, D_MODEL), pick, pipeline_mode=single),
    ]
    args += [g, wg, wu, wd]
    return pl.pallas_call(
        functools.partial(_ffn_body, mixer=mixer is not None),
        grid=(t // FFN_TM,),
        in_specs=in_specs,
        out_specs=pl.BlockSpec((FFN_TM, D_MODEL), row),
        out_shape=jax.ShapeDtypeStruct((t, D_MODEL), F32),
        scratch_shapes=scratch,
        compiler_params=pltpu.CompilerParams(
            dimension_semantics=("arbitrary",), vmem_limit_bytes=VMEM_LIMIT),
        name="ffn_mix" if mixer is not None else "ffn",
    )(*args)


EV_TM = 1024
EV_PART = 256
POOL_HALO = 16


def _head_norm_rot(t, gn, ones_bd, c128, s128):
    ssq = jnp.dot((t * t).astype(BF16), ones_bd, preferred_element_type=F32)
    y = t * lax.rsqrt(ssq * (1.0 / HEAD_DIM) + EPS) * gn
    lane = lax.broadcasted_iota(jnp.int32, (1, LANES), 1) % HEAD_DIM
    outs = []
    for j in range(D_ATTN // LANES):
        yj = y[:, j * LANES:(j + 1) * LANES]
        swapped = jnp.where(lane < ROT_DIM // 2,
                            pltpu.roll(yj, LANES - ROT_DIM // 2, 1),
                            pltpu.roll(yj, ROT_DIM // 2, 1))
        outs.append(yj * c128 + swapped * s128)
    return outs


def _evproj_body(x_ref, g_ref, w_ref, qn_ref, kn_ref, ones_ref, cs_ref, e_ref,
                 pw_ref, ps_ref, q_ref, k_ref, v_ref, p_ref, carry_ref, stage_ref):
    it = pl.program_id(1)

    @pl.when(it == 0)
    def _():
        carry_ref[...] = jnp.zeros_like(carry_ref)

    ones_bd = ones_ref[...]
    lane = lax.broadcasted_iota(jnp.int32, (1, LANES), 1) % HEAD_DIM
    slabs = D_ATTN // LANES

    def part(pi):
        r0 = pi * EV_PART
        rows = slice(r0, r0 + EV_PART)
        sub = EV_PART // ATT_RES
        h = _rms(x_ref[rows, :], g_ref[...]).astype(BF16)
        proj = jnp.dot(h, w_ref[...], preferred_element_type=F32)
        cs = cs_ref[rows, :]
        cs_hi = cs.astype(BF16)
        cs_lo = (cs - cs_hi.astype(F32)).astype(BF16)
        spread = (jnp.dot(cs_hi, e_ref[...], preferred_element_type=F32)
                  + jnp.dot(cs_lo, e_ref[...], preferred_element_type=F32))
        c128 = spread[:, :LANES] + jnp.where(lane < ROT_DIM, 0.0, 1.0)
        s128 = spread[:, LANES:]
        qs = _head_norm_rot(proj[:, :D_ATTN], qn_ref[...], ones_bd, c128, s128)
        ks = _head_norm_rot(proj[:, D_ATTN:2 * D_ATTN], kn_ref[...], ones_bd, c128, s128)
        vs = [proj[:, 2 * D_ATTN + j * LANES:2 * D_ATTN + (j + 1) * LANES] for j in range(slabs)]
        for a, (o_ref, vals) in enumerate(((q_ref, qs), (k_ref, ks), (v_ref, vs))):
            for j in range(slabs):
                stage = stage_ref.at[(pi * 3 + a) * slabs + j]
                stage[...] = vals[j]
                for r in range(ATT_RES):
                    o_ref[r, pi * sub:(pi + 1) * sub, j * LANES:(j + 1) * LANES] = stage[
                        pl.ds(r, sub, stride=ATT_RES), :]

        p = proj[:, 3 * D_ATTN:]
        pe = jnp.concatenate([carry_ref[...], p], axis=0)
        carry_ref[...] = p[EV_PART - POOL_HALO:, :]
        pos1 = (it * EV_TM + r0 + 1 + lax.broadcasted_iota(jnp.int32, (EV_PART, 1), 0)).astype(F32)
        for g, w in enumerate(POOL_WINDOWS):
            sl = slice(g * POOL_C, (g + 1) * POOL_C)
            s = pe[:, sl]
            shift = 1
            while shift < w:
                s = s + pltpu.roll(s, shift, 0)
                shift *= 2
            count = jnp.minimum(pos1, float(w))
            pooled = s[POOL_HALO:, :] / count - p[:, sl]
            y = jnp.dot(pooled.astype(BF16), pw_ref[g], preferred_element_type=F32)
            p_ref[rows, sl] = y * ps_ref[:, sl]

    for pi in range(EV_TM // EV_PART):
        part(pi)


def _evproj(x, g, w_in, j, qn, kn, ones_bd, cs, spread, pool_w, pool_scale):
    b, s, _ = x.shape
    tok = lambda bi, i: (bi, i, 0)
    out = jax.ShapeDtypeStruct((b, s, D_ATTN), F32)
    res_out = jax.ShapeDtypeStruct((b, ATT_RES, s // ATT_RES, D_ATTN), F32)
    res_spec = pl.BlockSpec((None, ATT_RES, EV_TM // ATT_RES, D_ATTN), lambda bi, i: (bi, 0, i, 0))
    single = pl.Buffered(1)
    return pl.pallas_call(
        _evproj_body,
        grid=(b, s // EV_TM),
        in_specs=[
            pl.BlockSpec((None, EV_TM, D_MODEL), tok),
            _const_spec((1, D_MODEL)),
            pl.BlockSpec((None, D_MODEL, 4 * D_ATTN), lambda bi, i: (j, 0, 0), pipeline_mode=single),
            _const_spec((1, D_ATTN)),
            _const_spec((1, D_ATTN)),
            _const_spec((D_ATTN, D_ATTN)),
            pl.BlockSpec((None, EV_TM, ROT_DIM), tok),
            _const_spec((ROT_DIM, 2 * LANES)),
            pl.BlockSpec((None, len(POOL_WINDOWS), POOL_C, POOL_C), lambda bi, i: (j, 0, 0, 0),
                         pipeline_mode=single),
            _const_spec((1, D_POOL)),
        ],
        out_specs=[res_spec] * 3 + [pl.BlockSpec((None, EV_TM, D_ATTN), tok)],
        out_shape=[res_out] * 3 + [out],
        scratch_shapes=[pltpu.VMEM((POOL_HALO, D_POOL), F32),
                        pltpu.VMEM((3 * D_ATTN // LANES * (EV_TM // EV_PART), EV_PART, LANES), F32)],
        compiler_params=pltpu.CompilerParams(
            dimension_semantics=("arbitrary", "arbitrary"), vmem_limit_bytes=VMEM_LIMIT),
        name="evproj",
    )(x, g, w_in, qn, kn, ones_bd, cs, spread, pool_w, pool_scale)


ATT_BLK = N_BACK


ATT_UNROLL = 32
ATT_QSCALE = HEAD_DIM ** -0.5 * math.log2(math.e)


def _attn_body(q_ref, k_ref, v_ref, o_ref, bias_ref, *stats, seq):
    lane = lax.broadcasted_iota(jnp.int32, (1, LANES), 1)
    head0 = lane < HEAD_DIM
    units = seq // ATT_BLK
    sub = seq // ATT_RES
    qc = ATT_BLK // ATT_RES
    kc = 2 * qc

    qi = lax.broadcasted_iota(jnp.int32, (ATT_BLK, 2 * ATT_BLK), 0)
    kj = lax.broadcasted_iota(jnp.int32, (ATT_BLK, 2 * ATT_BLK), 1)
    qpos = (qi % qc) * ATT_RES + qi // qc
    kpos = (kj % kc) * ATT_RES + kj // kc
    for idx, (dq, off) in enumerate(((qi - kj, 0), (qi - kj, ATT_BLK),
                                     (qpos - kpos, 0), (qpos - kpos, ATT_BLK))):
        delta = dq + off
        bias_ref[idx] = jnp.where((delta >= 0) & (delta <= N_BACK), 0.0, NEG_BIG)

    ones = jnp.ones((2 * ATT_BLK, LANES), BF16)

    def attend(q2, k2, v2, bias):
        q2 = q2 * ATT_QSCALE
        qq = jnp.concatenate([jnp.where(head0, q2, 0.0), jnp.where(head0, 0.0, q2)],
                             axis=0).astype(BF16)
        vaug = jnp.concatenate([v2.astype(BF16), ones], axis=1)
        sc = lax.dot_general(qq, k2.astype(BF16), (((1,), (1,)), ((), ())),
                             preferred_element_type=F32)
        sc = sc + jnp.concatenate([bias, bias], axis=0)
        m = jnp.max(sc, axis=-1, keepdims=True)
        p = jnp.exp2(sc - m).astype(BF16)
        res = jnp.dot(p, vaug, preferred_element_type=F32)
        return (jnp.where(head0, res[:ATT_BLK, :LANES], res[ATT_BLK:, :LANES]),
                jnp.where(head0, res[:ATT_BLK, LANES:], res[ATT_BLK:, LANES:]),
                jnp.where(head0, m[:ATT_BLK], m[ATT_BLK:]))

    def unit_consecutive(nb, refs):
        q0 = pl.multiple_of(nb * qc, qc)
        k0 = pl.multiple_of(jnp.maximum(nb - 1, 0) * qc, qc)
        q2 = jnp.concatenate([q_ref[c, pl.ds(q0, qc), :] for c in range(ATT_RES)], axis=0)
        k2 = jnp.concatenate([k_ref[c, pl.ds(k0, kc), :] for c in range(ATT_RES)], axis=0)
        v2 = jnp.concatenate([v_ref[c, pl.ds(k0, kc), :] for c in range(ATT_RES)], axis=0)
        outs = attend(q2, k2, v2, bias_ref[2 + jnp.minimum(nb, 1)])
        for ref, val in zip(refs, outs):
            for c in range(ATT_RES):
                ref[c, pl.ds(q0, qc), :] = val[c * qc:(c + 1) * qc]

    def unit_strided(u, refs, step):
        per_res = units // ATT_RES
        r = u // per_res
        w = u % per_res
        nblk = per_res // step
        rr = w // nblk
        nb = w % nblk
        kb = jnp.maximum(nb - 1, 0)

        def rows(blk, n):
            start = rr + step * ATT_BLK * blk
            return pl.ds(start, n) if step == 1 else pl.ds(start, n, stride=step)

        qrows = rows(nb, ATT_BLK)
        krows = rows(kb, 2 * ATT_BLK)
        outs = attend(q_ref[r, qrows, :], k_ref[r, krows, :], v_ref[r, krows, :],
                      bias_ref[jnp.minimum(nb, 1)])
        for ref, val in zip(refs, outs):
            ref[r, qrows, :] = val

    for pi, d in enumerate(DILATIONS):
        refs = stats[3 * pi:3 * pi + 3]

        def group(it, carry, d=d, refs=refs):
            for c in range(ATT_UNROLL):
                u = it * ATT_UNROLL + c
                if d == 1:
                    unit_consecutive(u, refs)
                else:
                    unit_strided(u, refs, d // ATT_RES)
            return carry

        lax.fori_loop(0, units // ATT_UNROLL, group, 0)

    def finish(i, carry):
        per_res = sub // ATT_BLK
        r = i // per_res
        row0 = pl.multiple_of((i % per_res) * ATT_BLK, ATT_BLK)
        rs = pl.ds(row0, ATT_BLK)
        ms = [stats[3 * pi + 2][r, rs, :] for pi in range(len(DILATIONS))]
        m_tot = functools.reduce(jnp.maximum, ms)
        ws = [jnp.exp2(m - m_tot) for m in ms]
        num = sum(w * stats[3 * pi][r, rs, :] for pi, w in enumerate(ws))
        den = sum(w * stats[3 * pi + 1][r, rs, :] for pi, w in enumerate(ws))
        o_ref[pl.ds(r + ATT_RES * row0, ATT_BLK, stride=ATT_RES), :] = num / den
        return carry

    lax.fori_loop(0, units, finish, 0, unroll=4)


def _attn(q, k, v):
    b, res, sub, _ = q.shape
    spec = pl.BlockSpec((None, res, sub, LANES), lambda bi, hp: (bi, 0, 0, hp))
    return pl.pallas_call(
        functools.partial(_attn_body, seq=res * sub),
        grid=(b, D_ATTN // LANES),
        in_specs=[spec] * 3,
        out_specs=pl.BlockSpec((None, res * sub, LANES), lambda bi, hp: (bi, 0, hp)),
        out_shape=jax.ShapeDtypeStruct((b, res * sub, D_ATTN), F32),
        scratch_shapes=([pltpu.VMEM((4, ATT_BLK, 2 * ATT_BLK), F32)]
                        + [pltpu.VMEM((res, sub, LANES), F32)] * (3 * len(DILATIONS))),
        compiler_params=pltpu.CompilerParams(
            dimension_semantics=("arbitrary", "arbitrary"), vmem_limit_bytes=VMEM_LIMIT),
        name="dilated_attn",
    )(q, k, v)


S5_CH = 2


def _cmul(ar, ai, br, bi):
    return ar * br - ai * bi, ar * bi + ai * br


def _s5prep_body(are_ref, aim_ref, ldt_ref, brt_ref, bit_ref, cre_ref, cim_ref,
                 lamk_re_ref, lamk_im_ref, win_re_ref, win_im_ref, wout_re_ref, wout_im_ref, kin_ref):
    lam_re = jnp.minimum(are_ref[...], -1e-4)
    lam_im = aim_ref[...]
    dt = jnp.exp(ldt_ref[...])
    mag = jnp.exp(lam_re * dt)
    lbr = mag * jnp.cos(lam_im * dt)
    lbi = mag * jnp.sin(lam_im * dt)
    den = lam_re * lam_re + lam_im * lam_im
    nre = lbr - 1.0
    cr = (nre * lam_re + lbi * lam_im) / den
    ci = (lbi * lam_re - nre * lam_im) / den
    bbr, bbi = _cmul(cr[:, None, :], ci[:, None, :], brt_ref[...], bit_ref[...])
    c_re = cre_ref[...]
    c_im = cim_ref[...]

    powers = [(jnp.ones_like(lbr), jnp.zeros_like(lbr))]
    for _ in range(S5_CH):
        powers.append(_cmul(powers[-1][0], powers[-1][1], lbr, lbi))
    lamk_re_ref[...], lamk_im_ref[...] = powers[S5_CH]

    contract_p = (((2,), (2,)), ((0,), (0,)))
    for i in range(S5_CH):
        pr, pi = powers[S5_CH - 1 - i]
        win_re_ref[i], win_im_ref[i] = _cmul(pr[:, None, :], pi[:, None, :], bbr, bbi)
        qr, qi = powers[i + 1]
        er, ei = _cmul(c_re, c_im, qr[:, None, :], qi[:, None, :])
        wout_re_ref[i] = er
        wout_im_ref[i] = -ei
        er, ei = _cmul(c_re, c_im, powers[i][0][:, None, :], powers[i][1][:, None, :])
        kin_ref[i] = (lax.dot_general(bbr, er, contract_p, precision=lax.Precision.HIGHEST,
                                      preferred_element_type=F32)
                      - lax.dot_general(bbi, ei, contract_p, precision=lax.Precision.HIGHEST,
                                        preferred_element_type=F32))


def _s5prep(a_re, a_im, log_dt, b_re_t, b_im_t, c_re, c_im):
    gp = jax.ShapeDtypeStruct((S5_GROUPS, S5_STATE), F32)
    kgcp = jax.ShapeDtypeStruct((S5_CH, S5_GROUPS, S5_GROUP, S5_STATE), F32)
    kgcc = jax.ShapeDtypeStruct((S5_CH, S5_GROUPS, S5_GROUP, S5_GROUP), F32)
    return pl.pallas_call(
        _s5prep_body,
        out_shape=[gp, gp, kgcp, kgcp, kgcp, kgcp, kgcc],
        name="s5prep",
    )(a_re, a_im, log_dt, b_re_t, b_im_t, c_re, c_im)


S5_LC = 64
S5_SLABS = D_MODEL // LANES
S5_HALF = LANES // S5_GROUP * S5_STATE


def _gelu_tanh(y):
    return 0.5 * y * (1.0 + jnp.tanh(math.sqrt(2.0 / math.pi) * (y + 0.044715 * (y * y * y))))


def _spread_groups(m, rep):
    w = m.shape[-1]
    acc = jnp.zeros((m.shape[0], rep.shape[1]), F32)
    for _ in range(3):
        part = m.astype(BF16)
        m = m - part.astype(F32)
        acc = acc + jnp.dot(part, rep, preferred_element_type=F32)
    row_group = lax.broadcasted_iota(jnp.int32, acc.shape, 0) // S5_GROUP
    col_group = lax.broadcasted_iota(jnp.int32, acc.shape, 1) // w
    return jnp.where(row_group == col_group, acc, 0.0)


def _s5_build_weights(cwin_ref, cwout_ref, ckin_ref, rep_p_ref, rep_c_ref, win_ref, wout_ref, tin_ref):
    rep_p = rep_p_ref[...]
    rep_c = rep_c_ref[...]

    def build(j, carry):
        for i in range(S5_CH):
            rows = slice(i * LANES, (i + 1) * LANES)
            blk = jnp.concatenate([_spread_groups(cwin_ref[0, i, j], rep_p),
                                   _spread_groups(cwin_ref[1, i, j], rep_p)], axis=1)
            win_ref[j, rows, :] = blk.astype(BF16)
            blk = jnp.concatenate([_spread_groups(cwout_ref[0, i, j], rep_p),
                                   _spread_groups(cwout_ref[1, i, j], rep_p)], axis=1)
            wout_ref[j, :, rows] = blk.T.astype(BF16)
        lag = [_spread_groups(ckin_ref[dl, j], rep_c).astype(BF16) for dl in range(S5_CH)]
        zero = jnp.zeros((LANES, LANES), BF16)
        for a in range(S5_CH):
            for b in range(S5_CH):
                tin_ref[j, a * LANES:(a + 1) * LANES, b * LANES:(b + 1) * LANES] = lag[b - a] if b >= a else zero
        return carry

    lax.fori_loop(0, S5_SLABS, build, 0)


def _s5_body(x_ref, g_ref, cwin_ref, cwout_ref, ckin_ref, rep_p_ref, rep_c_ref, lamk_ref, d_ref, wglu_ref,
             o_ref, win_ref, wout_ref, tin_ref, utb_ref, st_ref, carry_ref, z_ref, gtb_ref, *, batch):
    nch = S5_LC // S5_CH
    crows = nch * batch
    rows = S5_LC * batch

    @pl.when(pl.program_id(0) == 0)
    def _():
        carry_ref[...] = jnp.zeros_like(carry_ref)
        _s5_build_weights(cwin_ref, cwout_ref, ckin_ref, rep_p_ref, rep_c_ref, win_ref, wout_ref, tin_ref)

    for b in range(batch):
        hb = _rms(x_ref[b], g_ref[...])
        for j in range(S5_SLABS):
            utb_ref[j, pl.ds(b, S5_LC, stride=batch), :] = hb[:, j * LANES:(j + 1) * LANES]

    for j in range(S5_SLABS):
        sl = slice(j * LANES, (j + 1) * LANES)
        u = utb_ref[j]
        u4 = u.reshape(nch, S5_CH, batch, LANES)
        ucat = jnp.concatenate([u4[:, i].reshape(crows, LANES) for i in range(S5_CH)],
                               axis=1).astype(BF16)
        st_ref[j] = jnp.dot(ucat, win_ref[j], preferred_element_type=F32)

        ar = lamk_ref[j, :, :S5_HALF]
        ai = lamk_ref[j, :, S5_HALF:]
        hr = carry_ref[j, :, :S5_HALF]
        hi = carry_ref[j, :, S5_HALF:]
        for c in range(nch):
            rs = pl.ds(c * batch, batch)
            gr = st_ref[j, rs, :S5_HALF]
            gi = st_ref[j, rs, S5_HALF:]
            st_ref[j, rs, :S5_HALF] = hr
            st_ref[j, rs, S5_HALF:] = hi
            hr, hi = ar * hr - ai * hi + gr, ar * hi + ai * hr + gi
        carry_ref[j, :, :S5_HALF] = hr
        carry_ref[j, :, S5_HALF:] = hi

        y4 = (jnp.dot(st_ref[j].astype(BF16), wout_ref[j], preferred_element_type=F32)
              + jnp.dot(ucat, tin_ref[j], preferred_element_type=F32))
        y = jnp.concatenate([y4[:, i * LANES:(i + 1) * LANES].reshape(nch, 1, batch, LANES)
                             for i in range(S5_CH)], axis=1).reshape(rows, LANES)
        y = y + d_ref[:, sl] * u
        z_ref[:, sl] = _gelu_tanh(y).astype(BF16)

    z = z_ref[...]
    for c in range(D_MODEL // MXU_COLS):
        val = jnp.dot(z, wglu_ref[:, c * MXU_COLS:(c + 1) * MXU_COLS], preferred_element_type=F32)
        gate = jnp.dot(z, wglu_ref[:, D_MODEL + c * MXU_COLS:D_MODEL + (c + 1) * MXU_COLS],
                       preferred_element_type=F32)
        glu = val * jax.nn.sigmoid(gate)
        for h in range(MXU_COLS // LANES):
            gtb_ref[c * (MXU_COLS // LANES) + h] = glu[:, h * LANES:(h + 1) * LANES]

    for b in range(batch):
        for j in range(S5_SLABS):
            sl = slice(j * LANES, (j + 1) * LANES)
            o_ref[b, :, sl] = x_ref[b, :, sl] + gtb_ref[j, pl.ds(b, S5_LC, stride=batch), :]


def _s5(x, g, cwin, cwout, ckin, rep_p, rep_c, lamk, d_skip, w_glu):
    b, s, _ = x.shape
    assert b == SUBLANES
    rows = S5_LC * b
    crows = rows // S5_CH
    blk = pl.BlockSpec((b, S5_LC, D_MODEL), lambda i: (0, i, 0))
    return pl.pallas_call(
        functools.partial(_s5_body, batch=b),
        grid=(s // S5_LC,),
        in_specs=[
            blk,
            _const_spec((1, D_MODEL)),
            _const_spec(cwin.shape),
            _const_spec(cwout.shape),
            _const_spec(ckin.shape),
            _const_spec(rep_p.shape),
            _const_spec(rep_c.shape),
            _const_spec((S5_SLABS, SUBLANES, 2 * S5_HALF)),
            _const_spec((1, D_MODEL)),
            _const_spec((D_MODEL, 2 * D_MODEL)),
        ],
        out_specs=blk,
        out_shape=jax.ShapeDtypeStruct(x.shape, F32),
        scratch_shapes=[
            pltpu.VMEM((S5_SLABS, S5_CH * LANES, 2 * S5_HALF), BF16),
            pltpu.VMEM((S5_SLABS, 2 * S5_HALF, S5_CH * LANES), BF16),
            pltpu.VMEM((S5_SLABS, S5_CH * LANES, S5_CH * LANES), BF16),
            pltpu.VMEM((S5_SLABS, rows, LANES), F32),
            pltpu.VMEM((S5_SLABS, crows, 2 * S5_HALF), F32),
            pltpu.VMEM((S5_SLABS, SUBLANES, 2 * S5_HALF), F32),
            pltpu.VMEM((rows, D_MODEL), BF16),
            pltpu.VMEM((S5_SLABS, rows, LANES), F32),
        ],
        compiler_params=pltpu.CompilerParams(
            dimension_semantics=("arbitrary",), vmem_limit_bytes=VMEM_LIMIT),
        name="s5_glu",
    )(x, g, cwin, cwout, ckin, rep_p, rep_c, lamk, d_skip, w_glu)


def _rotary_tables(positions):
    half = ROT_DIM // 2
    inv_freq = ROPE_THETA ** (-jnp.arange(0, ROT_DIM, 2, dtype=F32) / ROT_DIM)
    ang = inv_freq[:, None, None] * positions.astype(F32)[None]
    cs = jnp.concatenate([jnp.cos(ang), jnp.sin(ang)], axis=0).transpose(1, 2, 0)
    lane = jnp.arange(LANES) % HEAD_DIM
    freq = jnp.arange(half)[:, None]
    first = (lane[None, :] == freq).astype(F32)
    second = (lane[None, :] == freq + half).astype(F32)
    zeros = jnp.zeros_like(first)
    spread = jnp.concatenate([
        jnp.concatenate([first + second, zeros], axis=1),
        jnp.concatenate([zeros, second - first], axis=1),
    ], axis=0).astype(BF16)
    return cs, spread


def _s5_operands(lamk_re, lamk_im, win_re, win_im, wout_re, wout_im, kin_t):
    gl = LANES // S5_GROUP
    by_slab = lambda m: m.reshape(m.shape[:-3] + (S5_SLABS, LANES, m.shape[-1]))
    cwin = by_slab(jnp.stack([win_re, win_im]))
    cwout = by_slab(jnp.stack([wout_re, wout_im]))
    ckin = by_slab(kin_t)
    rep_p = jnp.tile(jnp.eye(S5_STATE, dtype=BF16), (1, gl))
    rep_c = jnp.tile(jnp.eye(S5_GROUP, dtype=BF16), (1, gl))
    lamk = jnp.concatenate([lamk_re.reshape(S5_SLABS, 1, S5_HALF), lamk_im.reshape(S5_SLABS, 1, S5_HALF)],
                           axis=-1)
    lamk = jnp.broadcast_to(lamk, (S5_SLABS, SUBLANES, 2 * S5_HALF))
    return cwin, cwout, ckin, rep_p, rep_c, lamk


def kernel(x, positions, ffn_norm, ffn_w_gate, ffn_w_up, ffn_w_down, mix_norm, ev_w_in, ev_q_norm, ev_k_norm, ev_pool_w, ev_pool_scale, ev_w_out, s5_a_re, s5_a_im, s5_log_dt, s5_b_re, s5_b_im, s5_c_re, s5_c_im, s5_d, s5_w_glu):
    b, s, d = x.shape
    t = b * s
    depth = ffn_norm.shape[0]
    cs, spread = _rotary_tables(positions)
    ones_bd = jnp.kron(jnp.eye(D_ATTN // HEAD_DIM, dtype=F32),
                       jnp.ones((HEAD_DIM, HEAD_DIM), F32)).astype(BF16)
    wg, wu, wd = ffn_w_gate.astype(BF16), ffn_w_up.astype(BF16), ffn_w_down.astype(BF16)
    w_in, pool_w, w_out = ev_w_in.astype(BF16), ev_pool_w.astype(BF16), ev_w_out.astype(BF16)

    def ffn(xx, layer, half, mixer=None):
        return _ffn(xx.reshape(t, d), ffn_norm[layer, half][None, :],
                    wg, wu, wd, layer, half, mixer).reshape(b, s, d)

    for layer in range(depth):
        x = ffn(x, layer, 0)
        j = layer // 2
        g = mix_norm[layer][None, :]
        mixer = None
        if layer % 2 == 0:
            q, k, v, pool = _evproj(
                x, g, w_in, j,
                jnp.tile(ev_q_norm[j], D_ATTN // HEAD_DIM)[None, :],
                jnp.tile(ev_k_norm[j], D_ATTN // HEAD_DIM)[None, :],
                ones_bd, cs, spread, pool_w, ev_pool_scale[j][None, :])
            mixer = (_attn(q, k, v).reshape(t, D_ATTN), pool.reshape(t, D_POOL), w_out, j)
        else:
            prep = _s5prep(
                s5_a_re[j], s5_a_im[j], s5_log_dt[j][:, None],
                s5_b_re[j].transpose(0, 2, 1), s5_b_im[j].transpose(0, 2, 1), s5_c_re[j], s5_c_im[j])
            x = _s5(x, g, *_s5_operands(*prep), s5_d[j][None, :], s5_w_glu[j].astype(BF16))
        x = ffn(x, layer, 1, mixer)
    return x
```
